```python
import jax
import jax.numpy as jnp
from jax import lax
import numpy as np

D_MODEL = 1024
BATCH = 2
SEQ = 16384
DEPTH = 4

RMS_EPS = 1e-6
SSD_HEADS = 16
SSD_HEAD_DIM = 64
SSD_INNER = SSD_HEADS * SSD_HEAD_DIM
SSD_STATE = 128
SSD_GROUPS = 4
SSD_CONV = 4
SSD_CHUNK = 128
SSD_CONV_DIM = SSD_INNER + 2 * SSD_GROUPS * SSD_STATE
GLA_HEADS = 4
GLA_DK = 64
GLA_DV = 128
GLA_KEY = GLA_HEADS * GLA_DK
GLA_VAL = GLA_HEADS * GLA_DV
GLA_RANK = 16
GLA_TAU = 16.0
GLA_CHUNK = 16
ATT_PATTERNS = ((128, 1), (512, 4), (2048, 16))
ATT_GROUPS = 3
ATT_HEADS_PER_GROUP = 4
ATT_HEADS = ATT_GROUPS * ATT_HEADS_PER_GROUP
ATT_HEAD_DIM = 128
ATT_OUT = ATT_HEADS_PER_GROUP * ATT_HEAD_DIM
ATT_BLOCK = 128
ROPE_THETA = 10000.0
N_BRANCH = 3
BRANCH_WIDTHS = (SSD_INNER, GLA_VAL, ATT_OUT)
BRANCH_TOTAL = SSD_INNER + GLA_VAL + ATT_OUT
IN_WIDTHS = (SSD_INNER, SSD_CONV_DIM, SSD_HEADS, GLA_KEY, GLA_KEY, GLA_VAL, GLA_RANK, GLA_VAL,
             3 * ATT_HEADS * ATT_HEAD_DIM, N_BRANCH * D_MODEL)
IN_COLS = sum(IN_WIDTHS)
FFN_HIDDEN = 2816
FFN_CONV = 3

kernel_name = 'hybrid_ssd_gla_dilated_convffn'


def _split_points(widths):
    return [int(v) for v in np.cumsum(widths)[:-1]]


def rmsnorm(x, w):
    xf = x.astype(jnp.float32)
    y = xf * lax.rsqrt(jnp.mean(xf * xf, axis=-1, keepdims=True) + RMS_EPS)
    return (y * w.astype(jnp.float32)).astype(x.dtype)


def group_rmsnorm(x, w, groups):
    shp = x.shape
    xg = x.reshape(shp[:-1] + (groups, shp[-1] // groups))
    xg = xg * lax.rsqrt(jnp.mean(xg * xg, axis=-1, keepdims=True) + RMS_EPS)
    return xg.reshape(shp) * w.astype(jnp.float32)


def causal_dwconv(x, w, b):
    width, ch = w.shape
    y = lax.conv_general_dilated(x, w[:, None, :].astype(x.dtype), (1,), [(width - 1, 0)],
                                 dimension_numbers=('NWC', 'WIO', 'NWC'), feature_group_count=ch)
    return y + b.astype(x.dtype)


def rope(x, pos):
    half = x.shape[-1] // 2
    inv = ROPE_THETA ** (-jnp.arange(half, dtype=jnp.float32) / half)
    ang = pos[:, None] * inv[None, :]
    cos = jnp.cos(ang)[:, None, :]
    sin = jnp.sin(ang)[:, None, :]
    x1, x2 = x[..., :half], x[..., half:]
    return jnp.concatenate([x1 * cos - x2 * sin, x2 * cos + x1 * sin], axis=-1)


def chunk_prefix_states(chunk_decay, chunk_states):
    def step(h, inp):
        dcy, st = inp
        return dcy * h + st, h
    init = jnp.zeros_like(chunk_states[:, 0])
    _, prev = lax.scan(step, init, (jnp.moveaxis(chunk_decay, 1, 0), jnp.moveaxis(chunk_states, 1, 0)))
    return jnp.moveaxis(prev, 0, 1)


def ssd_chunked_scan(x, a, b, c):
    bsz, seq, nh, p = x.shape
    g, n = b.shape[2], b.shape[3]
    hg = nh // g
    L = SSD_CHUNK
    nc = seq // L
    x = x.reshape(bsz, nc, L, g, hg, p)
    a = a.reshape(bsz, nc, L, g, hg)
    b = b.reshape(bsz, nc, L, g, n)
    c = c.reshape(bsz, nc, L, g, n)
    a_cum = jnp.cumsum(a, axis=2)
    causal = jnp.tril(jnp.ones((L, L), dtype=bool))
    seg = a_cum[:, :, :, None] - a_cum[:, :, None, :]
    decay = jnp.exp(jnp.where(causal[:, :, None, None], seg, -jnp.inf))
    cb = jnp.einsum('bctgn,bcsgn->bctsg', c, b)
    y_diag = jnp.einsum('bctsg,bctsgh,bcsghp->bctghp', cb, decay, x)
    to_end = jnp.exp(a_cum[:, :, -1:] - a_cum)
    states = jnp.einsum('bcsgn,bcsgh,bcsghp->bcghpn', b, to_end, x)
    prev = chunk_prefix_states(jnp.exp(a_cum[:, :, -1])[..., None, None], states)
    y_off = jnp.einsum('bctgn,bcghpn,bctgh->bctghp', c, prev, jnp.exp(a_cum))
    return (y_diag + y_off).reshape(bsz, seq, nh, p)


def ssd_mixer(z, xbc, dt_raw, conv_w, conv_b, dt_bias, a_log, d_skip, norm_w):
    bsz, seq, _ = z.shape
    xbc = jax.nn.silu(causal_dwconv(xbc, conv_w, conv_b)).astype(jnp.float32)
    xs, bm, cm = jnp.split(xbc, [SSD_INNER, SSD_INNER + SSD_GROUPS * SSD_STATE], axis=-1)
    xs = xs.reshape(bsz, seq, SSD_HEADS, SSD_HEAD_DIM)
    bm = bm.reshape(bsz, seq, SSD_GROUPS, SSD_STATE)
    cm = cm.reshape(bsz, seq, SSD_GROUPS, SSD_STATE)
    dt = jax.nn.softplus(dt_raw.astype(jnp.float32) + dt_bias.astype(jnp.float32))
    a = -jnp.exp(a_log.astype(jnp.float32))
    y = ssd_chunked_scan(xs * dt[..., None], dt * a, bm, cm)
    y = y + d_skip.astype(jnp.float32)[:, None] * xs
    y = y.reshape(bsz, seq, SSD_INNER) * jax.nn.silu(z.astype(jnp.float32))
    y = group_rmsnorm(y, norm_w, SSD_GROUPS)
    return y.astype(z.dtype)


def gla_chunked(q, k, v, log_a):
    bsz, seq, nh, dk = q.shape
    dv = v.shape[-1]
    C = GLA_CHUNK
    nc = seq // C
    q = q.reshape(bsz, nc, C, nh, dk)
    k = k.reshape(bsz, nc, C, nh, dk)
    v = v.reshape(bsz, nc, C, nh, dv)
    bcum = jnp.cumsum(log_a.reshape(bsz, nc, C, nh, dk), axis=2)
    causal = jnp.tril(jnp.ones((C, C), dtype=bool))
    diff = bcum[:, :, :, None] - bcum[:, :, None, :]
    decay = jnp.exp(jnp.where(causal[:, :, None, None], diff, -jnp.inf))
    scores = jnp.einsum('bcthd,bcshd,bctshd->bchts', q, k, decay)
    o_intra = jnp.einsum('bchts,bcshe->bcthe', scores, v)
    k_end = k * jnp.exp(bcum[:, :, -1:] - bcum)
    states = jnp.einsum('bcshd,bcshe->bchde', k_end, v)
    prev = chunk_prefix_states(jnp.exp(bcum[:, :, -1])[..., None], states)
    o_inter = jnp.einsum('bcthd,bchde->bcthe', q * jnp.exp(bcum), prev)
    return (o_intra + o_inter).reshape(bsz, seq, nh, dv)


def gla_mixer(q, k, v, gate_lr, r, gate_w, gate_b, norm_w):
    bsz, seq, _ = q.shape
    f32 = jnp.float32
    qh = q.astype(f32).reshape(bsz, seq, GLA_HEADS, GLA_DK) * (GLA_DK ** -0.5)
    kh = k.astype(f32).reshape(bsz, seq, GLA_HEADS, GLA_DK)
    vh = v.astype(f32).reshape(bsz, seq, GLA_HEADS, GLA_DV)
    pre = gate_lr.astype(f32) @ gate_w.astype(f32) + gate_b.astype(f32)
    log_a = (jax.nn.log_sigmoid(pre) / GLA_TAU).reshape(bsz, seq, GLA_HEADS, GLA_DK)
    o = gla_chunked(qh, kh, vh, log_a)
    o = rmsnorm(o, norm_w) * jax.nn.silu(r.astype(f32).reshape(bsz, seq, GLA_HEADS, GLA_DV))
    return o.reshape(bsz, seq, GLA_VAL).astype(q.dtype)


def dilated_window_attention(q, k, v, window, dilation):
    bsz, seq, nh, dh = q.shape
    w_sub = window // dilation
    span = dilation * ATT_BLOCK
    seq_pad = -(-seq // span) * span
    L = seq_pad // dilation
    nb = L // ATT_BLOCK

    def to_sub(t):
        t = jnp.pad(t, ((0, 0), (0, seq_pad - seq), (0, 0), (0, 0)))
        t = t.reshape(bsz, L, dilation, nh, dh).transpose(0, 2, 1, 3, 4)
        return t.reshape(bsz * dilation, nb, ATT_BLOCK, nh, dh)

    def with_prev(t):
        prev = jnp.pad(t[:, :-1], ((0, 0), (1, 0), (0, 0), (0, 0), (0, 0)))
        return jnp.concatenate([prev, t], axis=2)

    qs = to_sub(q)
    ks = with_prev(to_sub(k))
    vs = with_prev(to_sub(v))
    s = jnp.einsum('bnqhd,bnkhd->bnhqk', qs, ks) * (dh ** -0.5)
    qi = jnp.arange(ATT_BLOCK)[:, None]
    ki = jnp.arange(2 * ATT_BLOCK)[None, :]
    dist = qi + ATT_BLOCK - ki
    blk = jnp.arange(nb)[:, None, None]
    valid = (dist >= 0) & (dist <= w_sub) & ((blk > 0) | (ki >= ATT_BLOCK))
    s = jnp.where(valid[None, :, None], s, -jnp.inf)
    lse = jax.nn.logsumexp(s, axis=-1)
    p = jnp.exp(s - lse[..., None])
    o = jnp.einsum('bnhqk,bnkhd->bnqhd', p, vs)
    o = o.reshape(bsz, dilation, L, nh, dh).transpose(0, 2, 1, 3, 4).reshape(bsz, seq_pad, nh, dh)[:, :seq]
    lse = lse.transpose(0, 1, 3, 2).reshape(bsz, dilation, L, nh).transpose(0, 2, 1, 3)
    lse = lse.reshape(bsz, seq_pad, nh)[:, :seq]
    return o, lse


def dilated_attention_mixer(qkv):
    bsz, seq, _ = qkv.shape
    t = qkv.astype(jnp.float32).reshape(bsz, seq, 3, ATT_HEADS, ATT_HEAD_DIM)
    pos = jnp.arange(seq, dtype=jnp.float32)
    shp = (bsz, seq, ATT_GROUPS, ATT_HEADS_PER_GROUP, ATT_HEAD_DIM)
    q = rope(t[:, :, 0], pos).reshape(shp)
    k = rope(t[:, :, 1], pos).reshape(shp)
    v = t[:, :, 2].reshape(shp)
    outs = []
    lses = []
    for g, (window, dilation) in enumerate(ATT_PATTERNS):
        o_g, lse_g = dilated_window_attention(q[:, :, g], k[:, :, g], v[:, :, g], window, dilation)
        outs.append(o_g)
        lses.append(lse_g)
    wts = jax.nn.softmax(jnp.stack(lses, axis=0), axis=0)
    o = jnp.sum(wts[..., None] * jnp.stack(outs, axis=0), axis=0)
    return o.reshape(bsz, seq, ATT_OUT).astype(qkv.dtype)


def hybrid_mixer(hn, w_in, ssd_conv_w, ssd_conv_b, ssd_dt_bias, ssd_a_log, ssd_d, ssd_norm_w,
                 gla_gate_w, gla_gate_b, gla_norm_w, w_branch, w_out):
    bsz, seq, _ = hn.shape
    proj = hn @ w_in.astype(hn.dtype)
    (z, xbc, dt_raw, gq, gk, gv, g_lr, g_r, qkv, gates) = jnp.split(proj, _split_points(IN_WIDTHS), axis=-1)
    y_ssd = ssd_mixer(z, xbc, dt_raw, ssd_conv_w, ssd_conv_b, ssd_dt_bias, ssd_a_log, ssd_d, ssd_norm_w)
    y_gla = gla_mixer(gq, gk, gv, g_lr, g_r, gla_gate_w, gla_gate_b, gla_norm_w)
    y_att = dilated_attention_mixer(qkv)
    gates = jax.nn.sigmoid(gates.astype(jnp.float32)).reshape(bsz, seq, N_BRANCH, D_MODEL).astype(hn.dtype)
    wb_ssd, wb_gla, wb_att = jnp.split(w_branch.astype(hn.dtype), _split_points(BRANCH_WIDTHS), axis=0)
    merged = (gates[:, :, 0] * (y_ssd @ wb_ssd)
              + gates[:, :, 1] * (y_gla @ wb_gla)
              + gates[:, :, 2] * (y_att @ wb_att))
    return merged @ w_out.astype(hn.dtype)


def conv_ffn(hn, up, conv_w, conv_b, down):
    gate, val = jnp.split(hn @ up.astype(hn.dtype), 2, axis=-1)
    gate = causal_dwconv(gate, conv_w, conv_b)
    return (jax.nn.silu(gate) * val) @ down.astype(hn.dtype)


def setup_inputs(seed: int = 0) -> dict:
    key = jax.random.key(seed)
    ks = jax.random.split(key, 24)
    f32 = jnp.float32

    def nrm(k, shape, scale):
        return jax.random.normal(k, shape, f32) * scale

    x = jax.random.normal(ks[0], (BATCH, SEQ, D_MODEL), f32)
    norm1_w = 1.0 + nrm(ks[1], (DEPTH, D_MODEL), 0.05)
    w_in = nrm(ks[2], (DEPTH, D_MODEL, IN_COLS), D_MODEL ** -0.5)
    ssd_conv_w = nrm(ks[3], (DEPTH, SSD_CONV, SSD_CONV_DIM), SSD_CONV ** -0.5)
    ssd_conv_b = nrm(ks[4], (DEPTH, SSD_CONV_DIM), 0.02)
    u = jax.random.uniform(ks[5], (DEPTH, SSD_HEADS), f32)
    dt0 = jnp.exp(u * (np.log(0.1) - np.log(0.001)) + np.log(0.001))
    ssd_dt_bias = dt0 + jnp.log(-jnp.expm1(-dt0))
    ssd_a_log = jnp.log(jax.random.uniform(ks[6], (DEPTH, SSD_HEADS), f32, 1.0, 16.0))
    ssd_d = 1.0 + nrm(ks[7], (DEPTH, SSD_HEADS), 0.1)
    ssd_norm_w = 1.0 + nrm(ks[8], (DEPTH, SSD_INNER), 0.05)
    gla_gate_w = nrm(ks[9], (DEPTH, GLA_RANK, GLA_KEY), GLA_RANK ** -0.5)
    gla_gate_b = nrm(ks[10], (DEPTH, GLA_KEY), 0.1)
    gla_norm_w = 1.0 + nrm(ks[11], (DEPTH, GLA_DV), 0.05)
    w_branch = jnp.concatenate([
        nrm(ks[12], (DEPTH, SSD_INNER, D_MODEL), SSD_INNER ** -0.5),
        nrm(ks[13], (DEPTH, GLA_VAL, D_MODEL), GLA_VAL ** -0.5),
        nrm(ks[14], (DEPTH, ATT_OUT, D_MODEL), ATT_OUT ** -0.5)], axis=1)
    w_out = nrm(ks[15], (DEPTH, D_MODEL, D_MODEL), D_MODEL ** -0.5)
    norm2_w = 1.0 + nrm(ks[16], (DEPTH, D_MODEL), 0.05)
    ffn_up = nrm(ks[17], (DEPTH, D_MODEL, 2 * FFN_HIDDEN), D_MODEL ** -0.5)
    ffn_conv_w = nrm(ks[18], (DEPTH, FFN_CONV, FFN_HIDDEN), FFN_CONV ** -0.5)
    ffn_conv_b = nrm(ks[19], (DEPTH, FFN_HIDDEN), 0.02)
    ffn_down = nrm(ks[20], (DEPTH, FFN_HIDDEN, D_MODEL), FFN_HIDDEN ** -0.5)
    final_norm_w = 1.0 + nrm(ks[21], (D_MODEL,), 0.05)
    return {'x': x, 'norm1_w': norm1_w, 'w_in': w_in, 'ssd_conv_w': ssd_conv_w,
            'ssd_conv_b': ssd_conv_b, 'ssd_dt_bias': ssd_dt_bias, 'ssd_a_log': ssd_a_log,
            'ssd_d': ssd_d, 'ssd_norm_w': ssd_norm_w, 'gla_gate_w': gla_gate_w,
            'gla_gate_b': gla_gate_b, 'gla_norm_w': gla_norm_w, 'w_branch': w_branch,
            'w_out': w_out, 'norm2_w': norm2_w, 'ffn_up': ffn_up, 'ffn_conv_w': ffn_conv_w,
            'ffn_conv_b': ffn_conv_b, 'ffn_down': ffn_down, 'final_norm_w': final_norm_w}


def reference(x, norm1_w, w_in, ssd_conv_w, ssd_conv_b, ssd_dt_bias, ssd_a_log, ssd_d, ssd_norm_w,
              gla_gate_w, gla_gate_b, gla_norm_w, w_branch, w_out, norm2_w, ffn_up, ffn_conv_w,
              ffn_conv_b, ffn_down, final_norm_w):
    for l in range(DEPTH):
        x = x + hybrid_mixer(rmsnorm(x, norm1_w[l]), w_in[l], ssd_conv_w[l], ssd_conv_b[l],
                             ssd_dt_bias[l], ssd_a_log[l], ssd_d[l], ssd_norm_w[l],
                             gla_gate_w[l], gla_gate_b[l], gla_norm_w[l], w_branch[l], w_out[l])
        x = x + conv_ffn(rmsnorm(x, norm2_w[l]), ffn_up[l], ffn_conv_w[l], ffn_conv_b[l], ffn_down[l])
    return rmsnorm(x, final_norm_w)
```

```python
import functools

import jax
import jax.numpy as jnp
import numpy as np
from jax import lax
from jax.experimental import pallas as pl
from jax.experimental.pallas import tpu as pltpu

F32 = jnp.float32
BF16 = jnp.bfloat16

LANES = 128
SUBLANES = 8
VMEM_LIMIT_BYTES = 56 * 1024 * 1024

RMS_EPS = 1e-6
SSD_HEADS = 16
SSD_HEAD_DIM = 64
SSD_INNER = SSD_HEADS * SSD_HEAD_DIM
SSD_STATE = 128
SSD_GROUPS = 4
SSD_CONV = 4
SSD_CONV_DIM = SSD_INNER + 2 * SSD_GROUPS * SSD_STATE
GLA_HEADS = 4
GLA_DK = 64
GLA_DV = 128
GLA_KEY = GLA_HEADS * GLA_DK
GLA_VAL = GLA_HEADS * GLA_DV
GLA_RANK = 16
GLA_TAU = 16.0
ATT_DILATIONS = (1, 4, 16)
ATT_WINDOWS = (128, 512, 2048)
ATT_GROUPS = 3
ATT_HPG = 4
ATT_HEADS = ATT_GROUPS * ATT_HPG
ATT_DH = 128
ATT_OUT = ATT_HPG * ATT_DH
ATT_BLOCK = 128
ROPE_THETA = 10000.0
FFN_CONV = 3

ROW_TILE = 512
SSD_CHUNK = 128
GLA_CHUNK = 128
ATT_ROWS = 256
SMALL_COLS = LANES

NEG_INF = float("-inf")


def _params(sem):
    return pltpu.CompilerParams(dimension_semantics=sem, vmem_limit_bytes=VMEM_LIMIT_BYTES)


def _const_spec(shape):
    zeros = (0,) * len(shape)
    return pl.BlockSpec(shape, lambda *_: zeros, pipeline_mode=pl.Buffered(1))


def _sigmoid(x):
    return 0.5 * jnp.tanh(0.5 * x) + 0.5


def _silu(x):
    return x * _sigmoid(x)


def _softplus(x):
    return jnp.maximum(x, 0.0) + jnp.log(1.0 + jnp.exp(-jnp.abs(x)))


def _dot(a, b):
    return jnp.dot(a, b, preferred_element_type=F32)


def _dot_nt(a, b):
    return lax.dot_general(a, b, (((1,), (1,)), ((), ())), preferred_element_type=F32)


def _dot_tn(a, b):
    return lax.dot_general(a, b, (((0,), (0,)), ((), ())), preferred_element_type=F32)


def _shift_rows(x, j, halo):
    rolled = pltpu.roll(x, j, 0)
    row = lax.broadcasted_iota(jnp.int32, (SUBLANES, x.shape[1]), 0)
    head = jnp.where(row < j, pltpu.roll(halo, j, 0), rolled[:SUBLANES])
    return jnp.concatenate([head, rolled[SUBLANES:]], axis=0)


def _cumsum_rows(x):
    n = x.shape[0]
    row = lax.broadcasted_iota(jnp.int32, x.shape, 0)
    k = 1
    while k < n:
        x = x + jnp.where(row >= k, pltpu.roll(x, k, 0), 0.0)
        k *= 2
    return x


def _rmsnorm_rows(x, w):
    return x * lax.rsqrt(jnp.mean(x * x, axis=-1, keepdims=True) + RMS_EPS) * w


def _rope_table_kernel(inv_ref, cos_ref, sin_ref):
    rows = cos_ref.shape[0]
    base = pl.program_id(0) * rows
    pos = (base + lax.broadcasted_iota(jnp.int32, (rows, ATT_DH), 0)).astype(F32)
    lane = lax.broadcasted_iota(jnp.int32, (rows, ATT_DH), 1)
    ang = pos * inv_ref[...]
    cos_ref[...] = jnp.cos(ang)
    s = jnp.sin(ang)
    sin_ref[...] = jnp.where(lane < ATT_DH // 2, -s, s)


def _rope_table(seq):
    half = ATT_DH // 2
    inv = ROPE_THETA ** (-jnp.arange(half, dtype=F32) / half)
    inv = jnp.concatenate([inv, inv]).reshape(1, ATT_DH)
    rows = min(seq, 1024)
    return pl.pallas_call(
        _rope_table_kernel,
        grid=(seq // rows,),
        in_specs=[pl.BlockSpec((1, ATT_DH), lambda i: (0, 0))],
        out_specs=[pl.BlockSpec((rows, ATT_DH), lambda i: (i, 0))] * 2,
        out_shape=[jax.ShapeDtypeStruct((seq, ATT_DH), F32)] * 2,
        compiler_params=_params(("parallel",)),
        name="rope_table",
    )(inv)


def _norm_kernel(x_ref, w_ref, h_ref):
    h_ref[...] = _rmsnorm_rows(x_ref[...], w_ref[...]).astype(h_ref.dtype)


def _first_norm(x2d, w):
    t, d = x2d.shape
    return pl.pallas_call(
        _norm_kernel,
        grid=(t // ROW_TILE,),
        in_specs=[pl.BlockSpec((ROW_TILE, d), lambda i: (i, 0)), _const_spec((1, d))],
        out_specs=pl.BlockSpec((ROW_TILE, d), lambda i: (i, 0)),
        out_shape=jax.ShapeDtypeStruct((t, d), BF16),
        compiler_params=_params(("parallel",)),
        name="first_norm",
    )(x2d, w.reshape(1, d))


def _proj_kernel(*refs, mode, col_tile, q_cols, q_scale):
    if mode == "rope":
        h_ref, w_ref, cos_ref, sin_ref, o_ref = refs
    else:
        h_ref, w_ref, o_ref = refs
    h = h_ref[...]
    n = w_ref.shape[1]
    for j in range(n // col_tile):
        cols = slice(j * col_tile, (j + 1) * col_tile)
        acc = _dot(h, w_ref[:, cols])
        if mode == "sigmoid":
            acc = _sigmoid(acc)
        elif mode == "rope":
            cos = cos_ref[...]
            sin = sin_ref[...]
            heads = []
            for a in range(col_tile // ATT_DH):
                xh = acc[:, a * ATT_DH:(a + 1) * ATT_DH]
                xh = xh * cos + pltpu.roll(xh, ATT_DH // 2, 1) * sin
                if j * col_tile + a * ATT_DH < q_cols:
                    xh = xh * q_scale
                heads.append(xh)
            acc = jnp.concatenate(heads, axis=1)
        o_ref[:, cols] = acc.astype(o_ref.dtype)


def _project(h, w, *, mode="plain", out_dtype=BF16, rope=None, seq=None, q_cols=0, q_scale=1.0):
    t, d = h.shape
    n = w.shape[1]
    col_tile = min(n, 512)
    in_specs = [pl.BlockSpec((ROW_TILE, d), lambda i: (i, 0)), _const_spec((d, n))]
    args = [h, w]
    if mode == "rope":
        tiles_per_seq = seq // ROW_TILE
        spec = pl.BlockSpec((ROW_TILE, ATT_DH), lambda i: (i % tiles_per_seq, 0))
        in_specs += [spec, spec]
        args += list(rope)
    return pl.pallas_call(
        functools.partial(_proj_kernel, mode=mode, col_tile=col_tile, q_cols=q_cols, q_scale=q_scale),
        grid=(t // ROW_TILE,),
        in_specs=in_specs,
        out_specs=pl.BlockSpec((ROW_TILE, n), lambda i: (i, 0)),
        out_shape=jax.ShapeDtypeStruct((t, n), out_dtype),
        compiler_params=_params(("parallel",)),
        name="proj_" + mode,
    )(*args)


def _ssd_kernel(xbc_ref, z_ref, sm_ref, cw_ref, cb_ref, dtb_ref, aneg_ref, dsk_ref, nw_ref,
                y_ref, halo_ref, st_ref):
    L = xbc_ref.shape[0]
    n_state = SSD_STATE
    pair = 2 * SSD_HEAD_DIM
    hpg = SSD_HEADS // SSD_GROUPS

    @pl.when(pl.program_id(1) == 0)
    def _():
        halo_ref[...] = jnp.zeros_like(halo_ref)
        st_ref[...] = jnp.zeros_like(st_ref)

    x_in = xbc_ref[...].astype(F32)
    halo = halo_ref[...]
    halo_ref[...] = x_in[L - SUBLANES:, :]
    cw = cw_ref[...]
    acc = x_in * cw[SSD_CONV - 1:SSD_CONV, :] + cb_ref[...]
    for j in range(1, SSD_CONV):
        acc = acc + _shift_rows(x_in, j, halo) * cw[SSD_CONV - 1 - j:SSD_CONV - j, :]
    xc = _silu(acc)
    xs = xc[:, :SSD_INNER]
    b_bf = xc[:, SSD_INNER:SSD_INNER + SSD_GROUPS * n_state].astype(BF16)
    cm = xc[:, SSD_INNER + SSD_GROUPS * n_state:]
    xs_bf = xs.astype(BF16)

    dt = _softplus(sm_ref[...] + dtb_ref[...])
    a_cum = _cumsum_rows(dt * aneg_ref[...])
    a_last = a_cum[L - 1:L, :]
    w_end = jnp.exp(a_last - a_cum) * dt
    e_cum = jnp.exp(a_cum)
    chunk_decay = jnp.exp(a_last)
    a_cum_t = a_cum.T
    dt_t = dt.T

    ti = lax.broadcasted_iota(jnp.int32, (L, L), 0)
    si = lax.broadcasted_iota(jnp.int32, (L, L), 1)
    causal = ti >= si
    lane = lax.broadcasted_iota(jnp.int32, (L, pair), 1)
    low = lane < SSD_HEAD_DIM

    y_blocks = []
    for p in range(SSD_HEADS // 2):
        g = (2 * p) // hpg
        b_g = b_bf[:, g * n_state:(g + 1) * n_state]
        c_g = cm[:, g * n_state:(g + 1) * n_state]
        cb = _dot_nt(c_g.astype(BF16), b_g)
        x_pair = xs_bf[:, p * pair:(p + 1) * pair]
        st_pair = st_ref[p]
        st_bf = st_pair.astype(BF16)
        ys = []
        for h in (2 * p, 2 * p + 1):
            seg = a_cum[:, h:h + 1] - a_cum_t[h:h + 1, :]
            dec = jnp.exp(jnp.where(causal, seg, NEG_INF))
            m = (cb * dec * dt_t[h:h + 1, :]).astype(BF16)
            c_s = (c_g * e_cum[:, h:h + 1]).astype(BF16)
            ys.append(_dot(m, x_pair) + _dot_nt(c_s, st_bf))
        y_blocks.append(jnp.where(low, ys[0], ys[1]))

        w_pair = jnp.where(low, w_end[:, 2 * p:2 * p + 1], w_end[:, 2 * p + 1:2 * p + 2])
        xw = (xs[:, p * pair:(p + 1) * pair] * w_pair).astype(BF16)
        rowi = lax.broadcasted_iota(jnp.int32, (pair, n_state), 0)
        cd = jnp.where(rowi < SSD_HEAD_DIM, chunk_decay[:, 2 * p:2 * p + 1],
                       chunk_decay[:, 2 * p + 1:2 * p + 2])
        st_ref[p] = st_pair * cd + _dot_tn(xw, b_g)

    y = jnp.concatenate(y_blocks, axis=1)
    y = y + dsk_ref[...] * xs
    y = y * _silu(z_ref[...].astype(F32))
    gw = SSD_INNER // SSD_GROUPS
    nw = nw_ref[...]
    for g in range(SSD_GROUPS):
        cols = slice(g * gw, (g + 1) * gw)
        y_ref[:, cols] = _rmsnorm_rows(y[:, cols], nw[:, cols]).astype(y_ref.dtype)


def _ssd_mixer(ssd_in, small, conv_w, conv_b, dt_bias, a_log, d_skip, norm_w, bsz, seq):
    L = SSD_CHUNK
    nc = seq // L
    pad = SMALL_COLS - SSD_HEADS
    dtb = jnp.pad(dt_bias.astype(F32), (0, pad)).reshape(1, SMALL_COLS)
    aneg = jnp.pad(-jnp.exp(a_log.astype(F32)), (0, pad)).reshape(1, SMALL_COLS)
    dsk = jnp.repeat(d_skip.astype(F32), SSD_HEAD_DIM).reshape(1, SSD_INNER)
    return pl.pallas_call(
        _ssd_kernel,
        grid=(bsz, nc),
        in_specs=[
            pl.BlockSpec((L, SSD_CONV_DIM), lambda b, c: (b * nc + c, 0)),
            pl.BlockSpec((L, SSD_INNER), lambda b, c: (b * nc + c, SSD_CONV_DIM // SSD_INNER)),
            pl.BlockSpec((L, SMALL_COLS), lambda b, c: (b * nc + c, 0)),
            _const_spec((SSD_CONV, SSD_CONV_DIM)),
            _const_spec((1, SSD_CONV_DIM)),
            _const_spec((1, SMALL_COLS)),
            _const_spec((1, SMALL_COLS)),
            _const_spec((1, SSD_INNER)),
            _const_spec((1, SSD_INNER)),
        ],
        out_specs=pl.BlockSpec((L, SSD_INNER), lambda b, c: (b * nc + c, 0)),
        out_shape=jax.ShapeDtypeStruct((bsz * seq, SSD_INNER), BF16),
        scratch_shapes=[
            pltpu.VMEM((SUBLANES, SSD_CONV_DIM), F32),
            pltpu.VMEM((SSD_HEADS // 2, 2 * SSD_HEAD_DIM, SSD_STATE), F32),
        ],
        compiler_params=_params(("parallel", "arbitrary")),
        name="ssd_mixer",
    )(ssd_in, ssd_in, small, conv_w.astype(F32), conv_b.astype(F32).reshape(1, -1), dtb, aneg, dsk,
      norm_w.astype(F32).reshape(1, -1))


def _gla_level_ref_rows(b_ref, m, C):
    blk = 2 * m
    width = b_ref.shape[1]
    if blk >= SUBLANES:
        parts = []
        for s in range(0, C, blk):
            row = b_ref[s + m - 1:s + m, :]
            parts.append(jnp.broadcast_to(row, (blk, width)))
        return parts[0] if len(parts) == 1 else jnp.concatenate(parts, axis=0)
    b = b_ref[...]
    off = lax.broadcasted_iota(jnp.int32, (C, width), 0) % blk
    out = b
    for o in range(blk):
        delta = o - (m - 1)
        if delta == 0:
            continue
        out = jnp.where(off == o, pltpu.roll(b, delta % C, 0), out)
    return out


def _gla_kernel(qk_ref, v_ref, r_ref, sm_ref, gw_ref, gb_ref, nw_ref, y_ref, st_ref, b_ref):
    C = qk_ref.shape[0]
    pair = 2 * GLA_DK

    @pl.when(pl.program_id(1) == 0)
    def _():
        st_ref[...] = jnp.zeros_like(st_ref)

    qk = qk_ref[...].astype(F32)
    q = qk[:, :GLA_KEY]
    k = qk[:, GLA_KEY:]

    s_f = sm_ref[...]
    s_hi = s_f.astype(BF16)
    s_lo = (s_f - s_hi.astype(F32)).astype(BF16)
    g_f = gw_ref[...]
    g_hi = g_f.astype(BF16)
    g_lo = (g_f - g_hi.astype(F32)).astype(BF16)
    pre = _dot(s_hi, g_hi) + _dot(s_hi, g_lo) + _dot(s_lo, g_hi) + gb_ref[...]
    log_a = -_softplus(-pre) * (1.0 / GLA_TAU)
    b = _cumsum_rows(log_a)
    b_ref[...] = b
    b_last = b[C - 1:C, :]

    rowi = lax.broadcasted_iota(jnp.int32, (C, GLA_KEY), 0)
    lane = lax.broadcasted_iota(jnp.int32, (C, pair), 1)
    head_lanes = (lane < GLA_DK, lane >= GLA_DK)
    ti = lax.broadcasted_iota(jnp.int32, (C, C), 0)
    si = lax.broadcasted_iota(jnp.int32, (C, C), 1)
    ts_xor = ti ^ si

    def head_ops(qa, ka, h):
        cols = slice((h // 2) * pair, (h // 2 + 1) * pair)
        sel = head_lanes[h % 2]
        qh = jnp.where(sel, qa[:, cols], 0.0).astype(BF16)
        kh = jnp.where(sel, ka[:, cols], 0.0).astype(BF16)
        return qh, kh

    scores = []
    for h in range(GLA_HEADS):
        qh, kh = head_ops(q, k, h)
        scores.append(jnp.where(ti == si, _dot_nt(qh, kh), 0.0))
    m = C // 2
    while m >= 1:
        blk = 2 * m
        upper = (rowi & m) != 0
        beta = _gla_level_ref_rows(b_ref, m, C)
        e = jnp.exp(jnp.where(upper, b - beta, beta - b))
        qt = jnp.where(upper, q * e, 0.0)
        kt = jnp.where(upper, 0.0, k * e)
        for h in range(GLA_HEADS):
            qh, kh = head_ops(qt, kt, h)
            p = _dot_nt(qh, kh)
            if blk < C:
                p = jnp.where(ts_xor < blk, p, 0.0)
            scores[h] = scores[h] + p
        m //= 2

    qe = q * jnp.exp(b)
    k_end = k * jnp.exp(b_last - b)
    st_decay = jnp.exp(b_last)
    nw = nw_ref[...]
    for h in range(GLA_HEADS):
        j = h // 2
        vh = v_ref[:, h * GLA_DV:(h + 1) * GLA_DV]
        st = st_ref[h]
        qh, kh = head_ops(qe, k_end, h)
        o = _dot(scores[h].astype(BF16), vh) + _dot_nt(qh, st.astype(BF16))
        st_ref[h] = st * st_decay[:, j * pair:(j + 1) * pair] + _dot_tn(vh, kh)
        o = _rmsnorm_rows(o, nw) * _silu(r_ref[:, h * GLA_DV:(h + 1) * GLA_DV].astype(F32))
        y_ref[:, h * GLA_DV:(h + 1) * GLA_DV] = o.astype(y_ref.dtype)


def _gla_mixer(gla_in, small, gate_w, gate_b, norm_w, bsz, seq):
    C = GLA_CHUNK
    nc = seq // C
    gw = jnp.zeros((SMALL_COLS, GLA_KEY), F32).at[SSD_HEADS:SSD_HEADS + GLA_RANK].set(gate_w.astype(F32))
    row = lambda col: (lambda b, c: (b * nc + c, col))
    return pl.pallas_call(
        _gla_kernel,
        grid=(bsz, nc),
        in_specs=[
            pl.BlockSpec((C, 2 * GLA_KEY), row(0)),
            pl.BlockSpec((C, GLA_VAL), row(1)),
            pl.BlockSpec((C, GLA_VAL), row(2)),
            pl.BlockSpec((C, SMALL_COLS), row(0)),
            _const_spec((SMALL_COLS, GLA_KEY)),
            _const_spec((1, GLA_KEY)),
            _const_spec((1, GLA_DV)),
        ],
        out_specs=pl.BlockSpec((C, GLA_VAL), row(0)),
        out_shape=jax.ShapeDtypeStruct((bsz * seq, GLA_VAL), BF16),
        scratch_shapes=[
            pltpu.VMEM((GLA_HEADS, GLA_DV, 2 * GLA_DK), F32),
            pltpu.VMEM((C, GLA_KEY), F32),
        ],
        compiler_params=_params(("parallel", "arbitrary")),
        name="gla_mixer",
    )(gla_in, gla_in, gla_in, small, gw, gate_b.astype(F32).reshape(1, -1),
      norm_w.astype(F32).reshape(1, -1))


def _att_kernel(q_ref, kh_ref, k_ref, vh_ref, v_ref, o_ref, lse_ref):
    R = q_ref.shape[0]
    blk = ATT_BLOCK
    not_first = pl.program_id(2) > 0
    qi = lax.broadcasted_iota(jnp.int32, (blk, blk), 0)
    ki = lax.broadcasted_iota(jnp.int32, (blk, blk), 1)
    prev_band = ki >= qi
    cur_band = ki <= qi
    lane = lax.broadcasted_iota(jnp.int32, (blk, LANES), 1)
    for sb in range(R // blk):
        rows = slice(sb * blk, (sb + 1) * blk)
        lse_tile = jnp.zeros((blk, LANES), F32)
        for h in range(ATT_HPG):
            cols = slice(h * ATT_DH, (h + 1) * ATT_DH)
            qh = q_ref[rows, cols]
            if sb == 0:
                kp, vp = kh_ref[:, cols], vh_ref[:, cols]
                pmask = jnp.logical_and(prev_band, not_first)
            else:
                prows = slice((sb - 1) * blk, sb * blk)
                kp, vp = k_ref[prows, cols], v_ref[prows, cols]
                pmask = prev_band
            s_p = jnp.where(pmask, _dot_nt(qh, kp), NEG_INF)
            s_c = jnp.where(cur_band, _dot_nt(qh, k_ref[rows, cols]), NEG_INF)
            mx = jnp.maximum(jnp.max(s_p, axis=-1, keepdims=True), jnp.max(s_c, axis=-1, keepdims=True))
            p_p = jnp.exp(s_p - mx)
            p_c = jnp.exp(s_c - mx)
            den = jnp.sum(p_p, axis=-1, keepdims=True) + jnp.sum(p_c, axis=-1, keepdims=True)
            o = _dot(p_p.astype(BF16), vp) + _dot(p_c.astype(BF16), v_ref[rows, cols])
            o_ref[rows, cols] = (o * (1.0 / den)).astype(o_ref.dtype)
            lse_tile = jnp.where(lane == h, mx + jnp.log(den), lse_tile)
        lse_ref[rows, :] = lse_tile


def _dilated_attention(qk, v, g, bsz, seq):
    d = ATT_DILATIONS[g]
    assert ATT_WINDOWS[g] // d == ATT_BLOCK and seq % (d * ATT_BLOCK) == 0
    sub = seq // d
    R = min(ATT_ROWS, sub)
    nb = sub // R
    per = R // ATT_BLOCK
    qk_v = qk.reshape(bsz, sub, d * 2 * ATT_GROUPS * ATT_OUT)
    v_v = v.reshape(bsz, sub, d * ATT_GROUPS * ATT_OUT)
    nq = 2 * ATT_GROUPS
    main = lambda col0, ncol: (lambda b, r, i: (b, i, r * ncol + col0))
    halo = lambda col0, ncol: (lambda b, r, i: (b, jnp.maximum(i * per - 1, 0), r * ncol + col0))
    o, lse = pl.pallas_call(
        _att_kernel,
        grid=(bsz, d, nb),
        in_specs=[
            pl.BlockSpec((None, R, ATT_OUT), main(g, nq)),
            pl.BlockSpec((None, ATT_BLOCK, ATT_OUT), halo(ATT_GROUPS + g, nq)),
            pl.BlockSpec((None, R, ATT_OUT), main(ATT_GROUPS + g, nq)),
            pl.BlockSpec((None, ATT_BLOCK, ATT_OUT), halo(g, ATT_GROUPS)),
            pl.BlockSpec((None, R, ATT_OUT), main(g, ATT_GROUPS)),
        ],
        out_specs=[
            pl.BlockSpec((None, R, ATT_OUT), lambda b, r, i: (b, i, r)),
            pl.BlockSpec((None, R, LANES), lambda b, r, i: (b, i, r)),
        ],
        out_shape=[
            jax.ShapeDtypeStruct((bsz, sub, d * ATT_OUT), BF16),
            jax.ShapeDtypeStruct((bsz, sub, d * LANES), F32),
        ],
        compiler_params=_params(("parallel", "parallel", "arbitrary")),
        name=f"dilated_attention_d{d}",
    )(qk_v, qk_v, qk_v, v_v, v_v)
    return o.reshape(bsz * seq, ATT_OUT), lse.reshape(bsz * seq, LANES)


def _merge_kernel(x_ref, ys_ref, yg_ref, o0_ref, o1_ref, o2_ref, l0_ref, l1_ref, l2_ref, gt_ref,
                  wb_ref, wo_ref, nw_ref, xo_ref, h_ref):
    rows = x_ref.shape[0]
    d = x_ref.shape[1]
    l0, l1, l2 = l0_ref[...], l1_ref[...], l2_ref[...]
    mx = jnp.maximum(jnp.maximum(l0, l1), l2)
    e0, e1, e2 = jnp.exp(l0 - mx), jnp.exp(l1 - mx), jnp.exp(l2 - mx)
    inv = 1.0 / (e0 + e1 + e2)
    wts = (e0 * inv, e1 * inv, e2 * inv)
    outs = (o0_ref, o1_ref, o2_ref)
    att = []
    for h in range(ATT_HPG):
        cols = slice(h * ATT_DH, (h + 1) * ATT_DH)
        acc = jnp.zeros((rows, ATT_DH), F32)
        for g in range(ATT_GROUPS):
            acc = acc + wts[g][:, h:h + 1] * outs[g][:, cols].astype(F32)
        att.append(acc.astype(BF16))
    y_att = jnp.concatenate(att, axis=1)

    merged = _sig_gate(gt_ref, 0, d) * _dot(ys_ref[...], wb_ref[:SSD_INNER, :])
    merged = merged + _sig_gate(gt_ref, 1, d) * _dot(yg_ref[...], wb_ref[SSD_INNER:SSD_INNER + GLA_VAL, :])
    merged = merged + _sig_gate(gt_ref, 2, d) * _dot(y_att, wb_ref[SSD_INNER + GLA_VAL:, :])
    x_new = x_ref[...] + _dot(merged.astype(BF16), wo_ref[...])
    xo_ref[...] = x_new
    h_ref[...] = _rmsnorm_rows(x_new, nw_ref[...]).astype(h_ref.dtype)


def _sig_gate(gt_ref, i, d):
    return gt_ref[:, i * d:(i + 1) * d].astype(F32)


def _merge(x2d, y_ssd, y_gla, att_o, att_lse, gates, w_branch, w_out, norm_w):
    t, d = x2d.shape
    row = lambda i: (i, 0)
    tile = lambda n: pl.BlockSpec((ROW_TILE, n), row)
    return pl.pallas_call(
        _merge_kernel,
        grid=(t // ROW_TILE,),
        in_specs=[tile(d), tile(SSD_INNER), tile(GLA_VAL)] + [tile(ATT_OUT)] * 3 + [tile(LANES)] * 3
        + [tile(3 * d), _const_spec(w_branch.shape), _const_spec(w_out.shape), _const_spec((1, d))],
        out_specs=[tile(d), tile(d)],
        out_shape=[jax.ShapeDtypeStruct((t, d), F32), jax.ShapeDtypeStruct((t, d), BF16)],
        compiler_params=_params(("parallel",)),
        name="merge_out",
    )(x2d, y_ssd, y_gla, *att_o, *att_lse, gates, w_branch, w_out, norm_w.astype(F32).reshape(1, d))


def _ffn_kernel(x_ref, h_ref, wg_ref, wv_ref, cw_ref, cb_ref, wd_ref, nw_ref, xo_ref, ho_ref,
                halo_ref, *, tiles_per_seq):
    rows = x_ref.shape[0]

    @pl.when(pl.program_id(0) % tiles_per_seq == 0)
    def _():
        halo_ref[...] = jnp.zeros_like(halo_ref)

    h = h_ref[...]
    gate = _dot(h, wg_ref[...])
    val = _dot(h, wv_ref[...])
    halo = halo_ref[...]
    halo_ref[...] = gate[rows - SUBLANES:, :]
    cw = cw_ref[...]
    acc = gate * cw[FFN_CONV - 1:FFN_CONV, :] + cb_ref[...]
    for j in range(1, FFN_CONV):
        acc = acc + _shift_rows(gate, j, halo) * cw[FFN_CONV - 1 - j:FFN_CONV - j, :]
    act = (_silu(acc) * val).astype(BF16)
    x_new = x_ref[...] + _dot(act, wd_ref[...])
    xo_ref[...] = x_new
    ho_ref[...] = _rmsnorm_rows(x_new, nw_ref[...]).astype(ho_ref.dtype)


def _conv_ffn(x2d, h, w_gate, w_val, conv_w, conv_b, w_down, norm_w, seq, out_dtype):
    t, d = x2d.shape
    hid = w_gate.shape[1]
    row = lambda i: (i, 0)
    tile = pl.BlockSpec((ROW_TILE, d), row)
    return pl.pallas_call(
        functools.partial(_ffn_kernel, tiles_per_seq=seq // ROW_TILE),
        grid=(t // ROW_TILE,),
        in_specs=[tile, tile, _const_spec((d, hid)), _const_spec((d, hid)), _const_spec((FFN_CONV, hid)),
                  _const_spec((1, hid)), _const_spec((hid, d)), _const_spec((1, d))],
        out_specs=[tile, tile],
        out_shape=[jax.ShapeDtypeStruct((t, d), F32), jax.ShapeDtypeStruct((t, d), out_dtype)],
        scratch_shapes=[pltpu.VMEM((SUBLANES, hid), F32)],
        compiler_params=_params(("arbitrary",)),
        name="conv_ffn",
    )(x2d, h, w_gate, w_val, conv_w.astype(F32), conv_b.astype(F32).reshape(1, hid), w_down,
      norm_w.astype(F32).reshape(1, d))


def _split_in_proj(w_in_l):
    widths = (SSD_INNER, SSD_CONV_DIM, SSD_HEADS, GLA_KEY, GLA_KEY, GLA_VAL, GLA_RANK, GLA_VAL,
              3 * ATT_HEADS * ATT_DH, None)
    pieces = []
    start = 0
    for wd in widths:
        stop = w_in_l.shape[1] if wd is None else start + wd
        pieces.append(w_in_l[:, start:stop])
        start = stop
    z, xbc, dt, gq, gk, gv, glr, gr, qkv, gates = pieces
    d = w_in_l.shape[0]
    qkv = qkv.reshape(d, 3, ATT_HEADS * ATT_DH)
    small = jnp.concatenate([dt, glr, jnp.zeros((d, SMALL_COLS - SSD_HEADS - GLA_RANK), w_in_l.dtype)], axis=1)
    cast = lambda w: w.astype(BF16)
    return {
        "ssd": cast(jnp.concatenate([xbc, z], axis=1)),
        "gla": cast(jnp.concatenate([gq * (GLA_DK ** -0.5), gk, gv, gr], axis=1)),
        "att_qk": cast(jnp.concatenate([qkv[:, 0], qkv[:, 1]], axis=1)),
        "att_v": cast(qkv[:, 2]),
        "gates": cast(gates),
        "small": cast(small),
    }


def kernel(x, norm1_w, w_in, ssd_conv_w, ssd_conv_b, ssd_dt_bias, ssd_a_log, ssd_d, ssd_norm_w,
           gla_gate_w, gla_gate_b, gla_norm_w, w_branch, w_out, norm2_w, ffn_up, ffn_conv_w,
           ffn_conv_b, ffn_down, final_norm_w):
    bsz, seq, d = x.shape
    depth = w_in.shape[0]
    hid = ffn_down.shape[1]
    assert seq % ROW_TILE == 0 and seq % SSD_CHUNK == 0 and seq % GLA_CHUNK == 0
    rope = _rope_table(seq)
    x2d = x.reshape(bsz * seq, d)
    h = _first_norm(x2d, norm1_w[0].astype(F32))
    for l in range(depth):
        w = _split_in_proj(w_in[l])
        ssd_in = _project(h, w["ssd"])
        gla_in = _project(h, w["gla"])
        att_qk = _project(h, w["att_qk"], mode="rope", rope=rope, seq=seq,
                          q_cols=ATT_HEADS * ATT_DH, q_scale=ATT_DH ** -0.5)
        att_v = _project(h, w["att_v"])
        gates = _project(h, w["gates"], mode="sigmoid")
        small = _project(h, w["small"], out_dtype=F32)

        y_ssd = _ssd_mixer(ssd_in, small, ssd_conv_w[l], ssd_conv_b[l], ssd_dt_bias[l], ssd_a_log[l],
                           ssd_d[l], ssd_norm_w[l], bsz, seq)
        y_gla = _gla_mixer(gla_in, small, gla_gate_w[l], gla_gate_b[l], gla_norm_w[l], bsz, seq)
        att = [_dilated_attention(att_qk, att_v, g, bsz, seq) for g in range(ATT_GROUPS)]
        x2d, h = _merge(x2d, y_ssd, y_gla, [a[0] for a in att], [a[1] for a in att], gates,
                        w_branch[l].astype(BF16), w_out[l].astype(BF16), norm2_w[l])

        last = l == depth - 1
        next_w = final_norm_w if last else norm1_w[l + 1]
        up = ffn_up[l].astype(BF16)
        x2d, h = _conv_ffn(x2d, h, up[:, :hid], up[:, hid:], ffn_conv_w[l], ffn_conv_b[l],
                           ffn_down[l].astype(BF16), next_w, seq, F32 if last else BF16)
    return h.reshape(bsz, seq, d)
```

```python
import functools

import jax
import jax.numpy as jnp
import numpy as np
from jax import lax
from jax.experimental import pallas as pl
from jax.experimental.pallas import tpu as pltpu

F32 = jnp.float32
BF16 = jnp.bfloat16

LANES = 128
SUBLANES = 8
VMEM_LIMIT_BYTES = 56 * 1024 * 1024

RMS_EPS = 1e-6
SSD_HEADS = 16
SSD_HEAD_DIM = 64
SSD_INNER = SSD_HEADS * SSD_HEAD_DIM
SSD_STATE = 128
SSD_GROUPS = 4
SSD_CONV = 4
SSD_CONV_DIM = SSD_INNER + 2 * SSD_GROUPS * SSD_STATE
GLA_HEADS = 4
GLA_DK = 64
GLA_DV = 128
GLA_KEY = GLA_HEADS * GLA_DK
GLA_VAL = GLA_HEADS * GLA_DV
GLA_RANK = 16
GLA_TAU = 16.0
ATT_DILATIONS = (1, 4, 16)
ATT_WINDOWS = (128, 512, 2048)
ATT_GROUPS = 3
ATT_HPG = 4
ATT_HEADS = ATT_GROUPS * ATT_HPG
ATT_DH = 128
ATT_OUT = ATT_HPG * ATT_DH
ATT_BLOCK = 128
ROPE_THETA = 10000.0
FFN_CONV = 3

ROW_TILE = 512
SSD_CHUNK = 128
GLA_CHUNK = 128
ATT_ROWS = 512
SMALL_COLS = LANES

NEG_INF = float("-inf")


def _params(sem):
    return pltpu.CompilerParams(dimension_semantics=sem, vmem_limit_bytes=VMEM_LIMIT_BYTES)


def _const_spec(shape):
    zeros = (0,) * len(shape)
    return pl.BlockSpec(shape, lambda *_: zeros, pipeline_mode=pl.Buffered(1))


def _sigmoid(x):
    return 0.5 * jnp.tanh(0.5 * x) + 0.5


def _silu(x):
    return x * _sigmoid(x)


def _softplus(x):
    return jnp.maximum(x, 0.0) + jnp.log(1.0 + jnp.exp(-jnp.abs(x)))


def _dot(a, b):
    return jnp.dot(a, b, preferred_element_type=F32)


def _dot_nt(a, b):
    return lax.dot_general(a, b, (((1,), (1,)), ((), ())), preferred_element_type=F32)


def _dot_tn(a, b):
    return lax.dot_general(a, b, (((0,), (0,)), ((), ())), preferred_element_type=F32)


def _shift_rows(x, j, halo):
    rolled = pltpu.roll(x, j, 0)
    row = lax.broadcasted_iota(jnp.int32, (SUBLANES, x.shape[1]), 0)
    head = jnp.where(row < j, pltpu.roll(halo, j, 0), rolled[:SUBLANES])
    return jnp.concatenate([head, rolled[SUBLANES:]], axis=0)


def _cumsum_rows(x):
    n = x.shape[0]
    row = lax.broadcasted_iota(jnp.int32, x.shape, 0)
    k = 1
    while k < n:
        x = x + jnp.where(row >= k, pltpu.roll(x, k, 0), 0.0)
        k *= 2
    return x


def _rmsnorm_rows(x, w):
    return x * lax.rsqrt(jnp.mean(x * x, axis=-1, keepdims=True) + RMS_EPS) * w


def _rope_table_kernel(inv_ref, cos_ref, sin_ref):
    rows = cos_ref.shape[0]
    base = pl.program_id(0) * rows
    pos = (base + lax.broadcasted_iota(jnp.int32, (rows, ATT_DH), 0)).astype(F32)
    lane = lax.broadcasted_iota(jnp.int32, (rows, ATT_DH), 1)
    ang = pos * inv_ref[...]
    cos_ref[...] = jnp.cos(ang)
    s = jnp.sin(ang)
    sin_ref[...] = jnp.where(lane < ATT_DH // 2, -s, s)


def _rope_table(seq):
    half = ATT_DH // 2
    inv = ROPE_THETA ** (-jnp.arange(half, dtype=F32) / half)
    inv = jnp.concatenate([inv, inv]).reshape(1, ATT_DH)
    rows = min(seq, 1024)
    return pl.pallas_call(
        _rope_table_kernel,
        grid=(seq // rows,),
        in_specs=[pl.BlockSpec((1, ATT_DH), lambda i: (0, 0))],
        out_specs=[pl.BlockSpec((rows, ATT_DH), lambda i: (i, 0))] * 2,
        out_shape=[jax.ShapeDtypeStruct((seq, ATT_DH), F32)] * 2,
        compiler_params=_params(("parallel",)),
        name="rope_table",
    )(inv)


def _norm_kernel(x_ref, w_ref, h_ref):
    h_ref[...] = _rmsnorm_rows(x_ref[...], w_ref[...]).astype(h_ref.dtype)


def _first_norm(x2d, w):
    t, d = x2d.shape
    return pl.pallas_call(
        _norm_kernel,
        grid=(t // ROW_TILE,),
        in_specs=[pl.BlockSpec((ROW_TILE, d), lambda i: (i, 0)), _const_spec((1, d))],
        out_specs=pl.BlockSpec((ROW_TILE, d), lambda i: (i, 0)),
        out_shape=jax.ShapeDtypeStruct((t, d), BF16),
        compiler_params=_params(("parallel",)),
        name="first_norm",
    )(x2d, w.reshape(1, d))


def _proj_kernel(h_ref, w_ref, o_ref, *, mode, col_tile):
    h = h_ref[...]
    n = w_ref.shape[1]
    for j in range(n // col_tile):
        cols = slice(j * col_tile, (j + 1) * col_tile)
        acc = _dot(h, w_ref[:, cols])
        if mode == "sigmoid":
            acc = _sigmoid(acc)
        o_ref[:, cols] = acc.astype(o_ref.dtype)


def _project(h, w, *, mode="plain", out_dtype=BF16):
    t, d = h.shape
    n = w.shape[1]
    col_tile = min(n, 512)
    return pl.pallas_call(
        functools.partial(_proj_kernel, mode=mode, col_tile=col_tile),
        grid=(t // ROW_TILE,),
        in_specs=[pl.BlockSpec((ROW_TILE, d), lambda i: (i, 0)), _const_spec((d, n))],
        out_specs=pl.BlockSpec((ROW_TILE, n), lambda i: (i, 0)),
        out_shape=jax.ShapeDtypeStruct((t, n), out_dtype),
        compiler_params=_params(("parallel",)),
        name="proj_" + mode,
    )(h, w)


def _att_proj_kernel(h_ref, w_ref, cos_ref, sin_ref, *rest):
    out_refs, scr_ref = rest[:-1], rest[-1]
    h = h_ref[...]
    rows = h.shape[0]
    cos = cos_ref[...]
    sin = sin_ref[...]
    for kind in range(3):
        for g, d in enumerate(ATT_DILATIONS):
            c0 = (kind * ATT_GROUPS + g) * ATT_OUT
            acc = _dot(h, w_ref[:, c0:c0 + ATT_OUT])
            o_ref = out_refs[kind * ATT_GROUPS + g]
            for a in range(ATT_HPG):
                xh = acc[:, a * ATT_DH:(a + 1) * ATT_DH]
                if kind < 2:
                    xh = xh * cos + pltpu.roll(xh, ATT_DH // 2, 1) * sin
                if kind == 0:
                    xh = xh * (ATT_DH ** -0.5)
                if d == 1:
                    o_ref[:, a * ATT_DH:(a + 1) * ATT_DH] = xh.astype(o_ref.dtype)
                    continue
                scr_ref[a * rows:(a + 1) * rows, :] = xh
                for r in range(d):
                    c = r * ATT_OUT + a * ATT_DH
                    o_ref[:, c:c + ATT_DH] = (
                        scr_ref[pl.ds(a * rows + r, rows // d, stride=d), :].astype(o_ref.dtype))


def _att_project(h, w_qkv, rope, seq):
    t, dm = h.shape
    tiles_per_seq = seq // ROW_TILE
    rope_spec = pl.BlockSpec((ROW_TILE, ATT_DH), lambda i: (i % tiles_per_seq, 0))
    out_specs, out_shape = [], []
    for _ in range(3):
        for d in ATT_DILATIONS:
            out_specs.append(pl.BlockSpec((ROW_TILE // d, d * ATT_OUT), lambda i: (i, 0)))
            out_shape.append(jax.ShapeDtypeStruct((t // d, d * ATT_OUT), BF16))
    return pl.pallas_call(
        _att_proj_kernel,
        grid=(t // ROW_TILE,),
        in_specs=[pl.BlockSpec((ROW_TILE, dm), lambda i: (i, 0)), _const_spec(w_qkv.shape),
                  rope_spec, rope_spec],
        out_specs=out_specs,
        out_shape=out_shape,
        scratch_shapes=[pltpu.VMEM((ATT_HPG * ROW_TILE, ATT_DH), F32)],
        compiler_params=_params(("parallel",)),
        name="proj_att",
    )(h, w_qkv, *rope)


def _ssd_kernel(xbc_ref, z_ref, sm_ref, cw_ref, cb_ref, dtb_ref, aneg_ref, dsk_ref, nw_ref,
                y_ref, halo_ref, st_ref):
    L = xbc_ref.shape[0]
    n_state = SSD_STATE
    pair = 2 * SSD_HEAD_DIM
    hpg = SSD_HEADS // SSD_GROUPS

    @pl.when(pl.program_id(1) == 0)
    def _():
        halo_ref[...] = jnp.zeros_like(halo_ref)
        st_ref[...] = jnp.zeros_like(st_ref)

    x_in = xbc_ref[...].astype(F32)
    halo = halo_ref[...]
    halo_ref[...] = x_in[L - SUBLANES:, :]
    cw = cw_ref[...]
    acc = x_in * cw[SSD_CONV - 1:SSD_CONV, :] + cb_ref[...]
    for j in range(1, SSD_CONV):
        acc = acc + _shift_rows(x_in, j, halo) * cw[SSD_CONV - 1 - j:SSD_CONV - j, :]
    xc = _silu(acc)
    xs = xc[:, :SSD_INNER]
    b_bf = xc[:, SSD_INNER:SSD_INNER + SSD_GROUPS * n_state].astype(BF16)
    cm = xc[:, SSD_INNER + SSD_GROUPS * n_state:]
    xs_bf = xs.astype(BF16)

    dt = _softplus(sm_ref[...] + dtb_ref[...])
    a_cum = _cumsum_rows(dt * aneg_ref[...])
    a_last = a_cum[L - 1:L, :]
    w_end = jnp.exp(a_last - a_cum) * dt
    e_cum = jnp.exp(a_cum)
    chunk_decay = jnp.exp(a_last)
    a_cum_t = a_cum.T
    dt_t = dt.T

    ti = lax.broadcasted_iota(jnp.int32, (L, L), 0)
    si = lax.broadcasted_iota(jnp.int32, (L, L), 1)
    causal = ti >= si
    lane = lax.broadcasted_iota(jnp.int32, (L, pair), 1)
    low = lane < SSD_HEAD_DIM

    y_blocks = []
    for p in range(SSD_HEADS // 2):
        g = (2 * p) // hpg
        b_g = b_bf[:, g * n_state:(g + 1) * n_state]
        c_g = cm[:, g * n_state:(g + 1) * n_state]
        cb = _dot_nt(c_g.astype(BF16), b_g)
        x_pair = xs_bf[:, p * pair:(p + 1) * pair]
        st_pair = st_ref[p]
        st_bf = st_pair.astype(BF16)
        ys = []
        for h in (2 * p, 2 * p + 1):
            seg = a_cum[:, h:h + 1] - a_cum_t[h:h + 1, :]
            dec = jnp.exp(jnp.where(causal, seg, NEG_INF))
            m = (cb * dec * dt_t[h:h + 1, :]).astype(BF16)
            c_s = (c_g * e_cum[:, h:h + 1]).astype(BF16)
            ys.append(_dot(m, x_pair) + _dot_nt(c_s, st_bf))
        y_blocks.append(jnp.where(low, ys[0], ys[1]))

        w_pair = jnp.where(low, w_end[:, 2 * p:2 * p + 1], w_end[:, 2 * p + 1:2 * p + 2])
        xw = (xs[:, p * pair:(p + 1) * pair] * w_pair).astype(BF16)
        rowi = lax.broadcasted_iota(jnp.int32, (pair, n_state), 0)
        cd = jnp.where(rowi < SSD_HEAD_DIM, chunk_decay[:, 2 * p:2 * p + 1],
                       chunk_decay[:, 2 * p + 1:2 * p + 2])
        st_ref[p] = st_pair * cd + _dot_tn(xw, b_g)

    y = jnp.concatenate(y_blocks, axis=1)
    y = y + dsk_ref[...] * xs
    y = y * _silu(z_ref[...].astype(F32))
    gw = SSD_INNER // SSD_GROUPS
    nw = nw_ref[...]
    for g in range(SSD_GROUPS):
        cols = slice(g * gw, (g + 1) * gw)
        y_ref[:, cols] = _rmsnorm_rows(y[:, cols], nw[:, cols]).astype(y_ref.dtype)


def _ssd_mixer(ssd_in, small, conv_w, conv_b, dt_bias, a_log, d_skip, norm_w, bsz, seq):
    L = SSD_CHUNK
    nc = seq // L
    pad = SMALL_COLS - SSD_HEADS
    dtb = jnp.pad(dt_bias.astype(F32), (0, pad)).reshape(1, SMALL_COLS)
    aneg = jnp.pad(-jnp.exp(a_log.astype(F32)), (0, pad)).reshape(1, SMALL_COLS)
    dsk = jnp.repeat(d_skip.astype(F32), SSD_HEAD_DIM).reshape(1, SSD_INNER)
    return pl.pallas_call(
        _ssd_kernel,
        grid=(bsz, nc),
        in_specs=[
            pl.BlockSpec((L, SSD_CONV_DIM), lambda b, c: (b * nc + c, 0)),
            pl.BlockSpec((L, SSD_INNER), lambda b, c: (b * nc + c, SSD_CONV_DIM // SSD_INNER)),
            pl.BlockSpec((L, SMALL_COLS), lambda b, c: (b * nc + c, 0)),
            _const_spec((SSD_CONV, SSD_CONV_DIM)),
            _const_spec((1, SSD_CONV_DIM)),
            _const_spec((1, SMALL_COLS)),
            _const_spec((1, SMALL_COLS)),
            _const_spec((1, SSD_INNER)),
            _const_spec((1, SSD_INNER)),
        ],
        out_specs=pl.BlockSpec((L, SSD_INNER), lambda b, c: (b * nc + c, 0)),
        out_shape=jax.ShapeDtypeStruct((bsz * seq, SSD_INNER), BF16),
        scratch_shapes=[
            pltpu.VMEM((SUBLANES, SSD_CONV_DIM), F32),
            pltpu.VMEM((SSD_HEADS // 2, 2 * SSD_HEAD_DIM, SSD_STATE), F32),
        ],
        compiler_params=_params(("parallel", "arbitrary")),
        name="ssd_mixer",
    )(ssd_in, ssd_in, small, conv_w.astype(F32), conv_b.astype(F32).reshape(1, -1), dtb, aneg, dsk,
      norm_w.astype(F32).reshape(1, -1))


def _gla_level_ref_rows(b_ref, m, C):
    blk = 2 * m
    width = b_ref.shape[1]
    if blk >= SUBLANES:
        parts = []
        for s in range(0, C, blk):
            row = b_ref[s + m - 1:s + m, :]
            parts.append(jnp.broadcast_to(row, (blk, width)))
        return parts[0] if len(parts) == 1 else jnp.concatenate(parts, axis=0)
    b = b_ref[...]
    off = lax.broadcasted_iota(jnp.int32, (C, width), 0) % blk
    out = b
    for o in range(blk):
        delta = o - (m - 1)
        if delta == 0:
            continue
        out = jnp.where(off == o, pltpu.roll(b, delta % C, 0), out)
    return out


def _gla_kernel(qk_ref, v_ref, r_ref, sm_ref, gw_ref, gb_ref, nw_ref, y_ref, st_ref, b_ref):
    C = qk_ref.shape[0]
    pair = 2 * GLA_DK

    @pl.when(pl.program_id(1) == 0)
    def _():
        st_ref[...] = jnp.zeros_like(st_ref)

    qk = qk_ref[...].astype(F32)
    q = qk[:, :GLA_KEY]
    k = qk[:, GLA_KEY:]

    s_f = sm_ref[...]
    s_hi = s_f.astype(BF16)
    s_lo = (s_f - s_hi.astype(F32)).astype(BF16)
    g_f = gw_ref[...]
    g_hi = g_f.astype(BF16)
    g_lo = (g_f - g_hi.astype(F32)).astype(BF16)
    pre = _dot(s_hi, g_hi) + _dot(s_hi, g_lo) + _dot(s_lo, g_hi) + gb_ref[...]
    log_a = -_softplus(-pre) * (1.0 / GLA_TAU)
    b = _cumsum_rows(log_a)
    b_ref[...] = b
    b_last = b[C - 1:C, :]

    rowi = lax.broadcasted_iota(jnp.int32, (C, GLA_KEY), 0)
    lane = lax.broadcasted_iota(jnp.int32, (C, pair), 1)
    head_lanes = (lane < GLA_DK, lane >= GLA_DK)
    ti = lax.broadcasted_iota(jnp.int32, (C, C), 0)
    si = lax.broadcasted_iota(jnp.int32, (C, C), 1)
    ts_xor = ti ^ si

    def head_ops(qa, ka, h):
        cols = slice((h // 2) * pair, (h // 2 + 1) * pair)
        sel = head_lanes[h % 2]
        qh = jnp.where(sel, qa[:, cols], 0.0).astype(BF16)
        kh = jnp.where(sel, ka[:, cols], 0.0).astype(BF16)
        return qh, kh

    scores = []
    for h in range(GLA_HEADS):
        qh, kh = head_ops(q, k, h)
        scores.append(jnp.where(ti == si, _dot_nt(qh, kh), 0.0))
    m = C // 2
    while m >= 1:
        blk = 2 * m
        upper = (rowi & m) != 0
        beta = _gla_level_ref_rows(b_ref, m, C)
        e = jnp.exp(jnp.where(upper, b - beta, beta - b))
        qt = jnp.where(upper, q * e, 0.0)
        kt = jnp.where(upper, 0.0, k * e)
        for h in range(GLA_HEADS):
            qh, kh = head_ops(qt, kt, h)
            p = _dot_nt(qh, kh)
            if blk < C:
                p = jnp.where(ts_xor < blk, p, 0.0)
            scores[h] = scores[h] + p
        m //= 2

    qe = q * jnp.exp(b)
    k_end = k * jnp.exp(b_last - b)
    st_decay = jnp.exp(b_last)
    nw = nw_ref[...]
    for h in range(GLA_HEADS):
        j = h // 2
        vh = v_ref[:, h * GLA_DV:(h + 1) * GLA_DV]
        st = st_ref[h]
        qh, kh = head_ops(qe, k_end, h)
        o = _dot(scores[h].astype(BF16), vh) + _dot_nt(qh, st.astype(BF16))
        st_ref[h] = st * st_decay[:, j * pair:(j + 1) * pair] + _dot_tn(vh, kh)
        o = _rmsnorm_rows(o, nw) * _silu(r_ref[:, h * GLA_DV:(h + 1) * GLA_DV].astype(F32))
        y_ref[:, h * GLA_DV:(h + 1) * GLA_DV] = o.astype(y_ref.dtype)


def _gla_mixer(gla_in, small, gate_w, gate_b, norm_w, bsz, seq):
    C = GLA_CHUNK
    nc = seq // C
    gw = jnp.zeros((SMALL_COLS, GLA_KEY), F32).at[SSD_HEADS:SSD_HEADS + GLA_RANK].set(gate_w.astype(F32))
    row = lambda col: (lambda b, c: (b * nc + c, col))
    return pl.pallas_call(
        _gla_kernel,
        grid=(bsz, nc),
        in_specs=[
            pl.BlockSpec((C, 2 * GLA_KEY), row(0)),
            pl.BlockSpec((C, GLA_VAL), row(1)),
            pl.BlockSpec((C, GLA_VAL), row(2)),
            pl.BlockSpec((C, SMALL_COLS), row(0)),
            _const_spec((SMALL_COLS, GLA_KEY)),
            _const_spec((1, GLA_KEY)),
            _const_spec((1, GLA_DV)),
        ],
        out_specs=pl.BlockSpec((C, GLA_VAL), row(0)),
        out_shape=jax.ShapeDtypeStruct((bsz * seq, GLA_VAL), BF16),
        scratch_shapes=[
            pltpu.VMEM((GLA_HEADS, GLA_DV, 2 * GLA_DK), F32),
            pltpu.VMEM((C, GLA_KEY), F32),
        ],
        compiler_params=_params(("parallel", "arbitrary")),
        name="gla_mixer",
    )(gla_in, gla_in, gla_in, small, gw, gate_b.astype(F32).reshape(1, -1),
      norm_w.astype(F32).reshape(1, -1))


def _att_kernel(q_ref, kh_ref, k_ref, vh_ref, v_ref, o_ref, lse_ref, kbuf_ref, vbuf_ref):
    R = q_ref.shape[0]
    blk = ATT_BLOCK
    kbuf_ref[:blk, :] = kh_ref[...]
    kbuf_ref[blk:, :] = k_ref[...]
    vbuf_ref[:blk, :] = vh_ref[...]
    vbuf_ref[blk:, :] = v_ref[...]
    qi = lax.broadcasted_iota(jnp.int32, (blk, 2 * blk), 0)
    kj = lax.broadcasted_iota(jnp.int32, (blk, 2 * blk), 1)
    band = jnp.abs(2 * (kj - qi) - blk) <= blk
    first_key = jnp.where(pl.program_id(2) > 0, 0, blk)
    lane = lax.broadcasted_iota(jnp.int32, (blk, LANES), 1)
    for sb in range(R // blk):
        rows = slice(sb * blk, (sb + 1) * blk)
        win = slice(sb * blk, (sb + 2) * blk)
        lse_tile = jnp.zeros((blk, LANES), F32)
        for h in range(ATT_HPG):
            cols = slice(h * ATT_DH, (h + 1) * ATT_DH)
            s = _dot_nt(q_ref[rows, cols], kbuf_ref[win, cols])
            s = jnp.where(band, s, NEG_INF)
            if sb == 0:
                s = jnp.where(kj >= first_key, s, NEG_INF)
            mx = jnp.max(s, axis=-1, keepdims=True)
            p = jnp.exp(s - mx)
            den = jnp.sum(p, axis=-1, keepdims=True)
            o = _dot(p.astype(BF16), vbuf_ref[win, cols])
            o_ref[rows, cols] = (o * (1.0 / den)).astype(o_ref.dtype)
            lse_tile = jnp.where(lane == h, mx + jnp.log(den), lse_tile)
        lse_ref[rows, :] = lse_tile


def _dilated_attention(q, k, v, g, bsz, seq):
    d = ATT_DILATIONS[g]
    assert ATT_WINDOWS[g] // d == ATT_BLOCK and seq % (d * ATT_BLOCK) == 0
    sub = seq // d
    R = min(ATT_ROWS, sub)
    nb = sub // R
    per = R // ATT_BLOCK
    main = pl.BlockSpec((R, ATT_OUT), lambda b, r, i: (b * nb + i, r))
    halo = pl.BlockSpec((ATT_BLOCK, ATT_OUT), lambda b, r, i: (jnp.maximum((b * nb + i) * per - 1, 0), r))
    return pl.pallas_call(
        _att_kernel,
        grid=(bsz, d, nb),
        in_specs=[main, halo, main, halo, main],
        out_specs=[main, pl.BlockSpec((R, LANES), lambda b, r, i: (b * nb + i, r))],
        out_shape=[
            jax.ShapeDtypeStruct((bsz * sub, d * ATT_OUT), BF16),
            jax.ShapeDtypeStruct((bsz * sub, d * LANES), F32),
        ],
        scratch_shapes=[pltpu.VMEM((R + ATT_BLOCK, ATT_OUT), BF16)] * 2,
        compiler_params=_params(("parallel", "parallel", "arbitrary")),
        name=f"dilated_attention_d{d}",
    )(q, k, k, v, v)


def _interleave_rows(src_ref, scr_ref, d, rows):
    n_blk = scr_ref.shape[0] // rows
    if d == 1:
        return [src_ref[:, a * LANES:(a + 1) * LANES].astype(F32) for a in range(n_blk)]
    for r in range(d):
        for a in range(n_blk):
            c = (r * n_blk + a) * LANES
            scr_ref[pl.ds(a * rows + r, rows // d, stride=d), :] = src_ref[:, c:c + LANES].astype(F32)
    return [scr_ref[a * rows:(a + 1) * rows, :] for a in range(n_blk)]


def _merge_kernel(x_ref, ys_ref, yg_ref, o0_ref, o1_ref, o2_ref, l0_ref, l1_ref, l2_ref, gt_ref,
                  wb_ref, wo_ref, nw_ref, xo_ref, h_ref, oscr_ref, lscr_ref):
    rows = x_ref.shape[0]
    d = x_ref.shape[1]
    lses = [_interleave_rows(l_ref, lscr_ref, dil, rows)[0]
            for l_ref, dil in zip((l0_ref, l1_ref, l2_ref), ATT_DILATIONS)]
    mx = jnp.maximum(jnp.maximum(lses[0], lses[1]), lses[2])
    es = [jnp.exp(l - mx) for l in lses]
    inv = 1.0 / (es[0] + es[1] + es[2])
    acc = [jnp.zeros((rows, ATT_DH), F32) for _ in range(ATT_HPG)]
    for g, o_ref in enumerate((o0_ref, o1_ref, o2_ref)):
        o_g = _interleave_rows(o_ref, oscr_ref, ATT_DILATIONS[g], rows)
        w_g = es[g] * inv
        for h in range(ATT_HPG):
            acc[h] = acc[h] + w_g[:, h:h + 1] * o_g[h]
    y_att = jnp.concatenate([a.astype(BF16) for a in acc], axis=1)

    merged = _sig_gate(gt_ref, 0, d) * _dot(ys_ref[...], wb_ref[:SSD_INNER, :])
    merged = merged + _sig_gate(gt_ref, 1, d) * _dot(yg_ref[...], wb_ref[SSD_INNER:SSD_INNER + GLA_VAL, :])
    merged = merged + _sig_gate(gt_ref, 2, d) * _dot(y_att, wb_ref[SSD_INNER + GLA_VAL:, :])
    x_new = x_ref[...] + _dot(merged.astype(BF16), wo_ref[...])
    xo_ref[...] = x_new
    h_ref[...] = _rmsnorm_rows(x_new, nw_ref[...]).astype(h_ref.dtype)


def _sig_gate(gt_ref, i, d):
    return gt_ref[:, i * d:(i + 1) * d].astype(F32)


def _merge(x2d, y_ssd, y_gla, att_o, att_lse, gates, w_branch, w_out, norm_w):
    t, d = x2d.shape
    row = lambda i: (i, 0)
    tile = lambda n: pl.BlockSpec((ROW_TILE, n), row)
    dil_tile = lambda n: [pl.BlockSpec((ROW_TILE // dil, dil * n), row) for dil in ATT_DILATIONS]
    return pl.pallas_call(
        _merge_kernel,
        grid=(t // ROW_TILE,),
        in_specs=[tile(d), tile(SSD_INNER), tile(GLA_VAL)] + dil_tile(ATT_OUT) + dil_tile(LANES)
        + [tile(3 * d), _const_spec(w_branch.shape), _const_spec(w_out.shape), _const_spec((1, d))],
        out_specs=[tile(d), tile(d)],
        out_shape=[jax.ShapeDtypeStruct((t, d), F32), jax.ShapeDtypeStruct((t, d), BF16)],
        scratch_shapes=[pltpu.VMEM((ATT_HPG * ROW_TILE, ATT_DH), F32), pltpu.VMEM((ROW_TILE, LANES), F32)],
        compiler_params=_params(("parallel",)),
        name="merge_out",
    )(x2d, y_ssd, y_gla, *att_o, *att_lse, gates, w_branch, w_out, norm_w.astype(F32).reshape(1, d))


def _ffn_kernel(x_ref, h_ref, wg_ref, wv_ref, cw_ref, cb_ref, wd_ref, nw_ref, xo_ref, ho_ref,
                halo_ref, *, tiles_per_seq):
    rows = x_ref.shape[0]

    @pl.when(pl.program_id(0) % tiles_per_seq == 0)
    def _():
        halo_ref[...] = jnp.zeros_like(halo_ref)

    h = h_ref[...]
    gate = _dot(h, wg_ref[...])
    val = _dot(h, wv_ref[...])
    halo = halo_ref[...]
    halo_ref[...] = gate[rows - SUBLANES:, :]
    cw = cw_ref[...]
    acc = gate * cw[FFN_CONV - 1:FFN_CONV, :] + cb_ref[...]
    for j in range(1, FFN_CONV):
        acc = acc + _shift_rows(gate, j, halo) * cw[FFN_CONV - 1 - j:FFN_CONV - j, :]
    act = (_silu(acc) * val).astype(BF16)
    x_new = x_ref[...] + _dot(act, wd_ref[...])
    xo_ref[...] = x_new
    ho_ref[...] = _rmsnorm_rows(x_new, nw_ref[...]).astype(ho_ref.dtype)


def _conv_ffn(x2d, h, w_gate, w_val, conv_w, conv_b, w_down, norm_w, seq, out_dtype):
    t, d = x2d.shape
    hid = w_gate.shape[1]
    row = lambda i: (i, 0)
    tile = pl.BlockSpec((ROW_TILE, d), row)
    return pl.pallas_call(
        functools.partial(_ffn_kernel, tiles_per_seq=seq // ROW_TILE),
        grid=(t // ROW_TILE,),
        in_specs=[tile, tile, _const_spec((d, hid)), _const_spec((d, hid)), _const_spec((FFN_CONV, hid)),
                  _const_spec((1, hid)), _const_spec((hid, d)), _const_spec((1, d))],
        out_specs=[tile, tile],
        out_shape=[jax.ShapeDtypeStruct((t, d), F32), jax.ShapeDtypeStruct((t, d), out_dtype)],
        scratch_shapes=[pltpu.VMEM((SUBLANES, hid), F32)],
        compiler_params=_params(("arbitrary",)),
        name="conv_ffn",
    )(x2d, h, w_gate, w_val, conv_w.astype(F32), conv_b.astype(F32).reshape(1, hid), w_down,
      norm_w.astype(F32).reshape(1, d))


def _split_in_proj(w_in_l):
    widths = (SSD_INNER, SSD_CONV_DIM, SSD_HEADS, GLA_KEY, GLA_KEY, GLA_VAL, GLA_RANK, GLA_VAL,
              3 * ATT_HEADS * ATT_DH, None)
    pieces = []
    start = 0
    for wd in widths:
        stop = w_in_l.shape[1] if wd is None else start + wd
        pieces.append(w_in_l[:, start:stop])
        start = stop
    z, xbc, dt, gq, gk, gv, glr, gr, qkv, gates = pieces
    d = w_in_l.shape[0]
    small = jnp.concatenate([dt, glr, jnp.zeros((d, SMALL_COLS - SSD_HEADS - GLA_RANK), w_in_l.dtype)], axis=1)
    cast = lambda w: w.astype(BF16)
    return {
        "ssd": cast(jnp.concatenate([xbc, z], axis=1)),
        "gla": cast(jnp.concatenate([gq * (GLA_DK ** -0.5), gk, gv, gr], axis=1)),
        "att": cast(qkv),
        "gates": cast(gates),
        "small": cast(small),
    }


def kernel(x, norm1_w, w_in, ssd_conv_w, ssd_conv_b, ssd_dt_bias, ssd_a_log, ssd_d, ssd_norm_w,
           gla_gate_w, gla_gate_b, gla_norm_w, w_branch, w_out, norm2_w, ffn_up, ffn_conv_w,
           ffn_conv_b, ffn_down, final_norm_w):
    bsz, seq, d = x.shape
    depth = w_in.shape[0]
    hid = ffn_down.shape[1]
    assert seq % ROW_TILE == 0 and seq % SSD_CHUNK == 0 and seq % GLA_CHUNK == 0
    rope = _rope_table(seq)
    x2d = x.reshape(bsz * seq, d)
    h = _first_norm(x2d, norm1_w[0].astype(F32))
    for l in range(depth):
        w = _split_in_proj(w_in[l])
        ssd_in = _project(h, w["ssd"])
        gla_in = _project(h, w["gla"])
        att_in = _att_project(h, w["att"], rope, seq)
        gates = _project(h, w["gates"], mode="sigmoid")
        small = _project(h, w["small"], out_dtype=F32)

        y_ssd = _ssd_mixer(ssd_in, small, ssd_conv_w[l], ssd_conv_b[l], ssd_dt_bias[l], ssd_a_log[l],
                           ssd_d[l], ssd_norm_w[l], bsz, seq)
        y_gla = _gla_mixer(gla_in, small, gla_gate_w[l], gla_gate_b[l], gla_norm_w[l], bsz, seq)
        att = [_dilated_attention(att_in[g], att_in[ATT_GROUPS + g], att_in[2 * ATT_GROUPS + g], g, bsz, seq)
               for g in range(ATT_GROUPS)]
        x2d, h = _merge(x2d, y_ssd, y_gla, [a[0] for a in att], [a[1] for a in att], gates,
                        w_branch[l].astype(BF16), w_out[l].astype(BF16), norm2_w[l])

        last = l == depth - 1
        next_w = final_norm_w if last else norm1_w[l + 1]
        up = ffn_up[l].astype(BF16)
        x2d, h = _conv_ffn(x2d, h, up[:, :hid], up[:, hid:], ffn_conv_w[l], ffn_conv_b[l],
                           ffn_down[l].astype(BF16), next_w, seq, F32 if last else BF16)
    return h.reshape(bsz, seq, d)
```

```python
import functools

import jax
import jax.numpy as jnp
import numpy as np
from jax import lax
from jax.experimental import pallas as pl
from jax.experimental.pallas import tpu as pltpu

F32 = jnp.float32
BF16 = jnp.bfloat16

LANES = 128
SUBLANES = 8
VMEM_LIMIT_BYTES = 56 * 1024 * 1024

RMS_EPS = 1e-6
SSD_HEADS = 16
SSD_HEAD_DIM = 64
SSD_INNER = SSD_HEADS * SSD_HEAD_DIM
SSD_STATE = 128
SSD_GROUPS = 4
SSD_CONV = 4
SSD_CONV_DIM = SSD_INNER + 2 * SSD_GROUPS * SSD_STATE
GLA_HEADS = 4
GLA_DK = 64
GLA_DV = 128
GLA_KEY = GLA_HEADS * GLA_DK
GLA_VAL = GLA_HEADS * GLA_DV
GLA_RANK = 16
GLA_TAU = 16.0
ATT_DILATIONS = (1, 4, 16)
ATT_WINDOWS = (128, 512, 2048)
ATT_GROUPS = 3
ATT_HPG = 4
ATT_HEADS = ATT_GROUPS * ATT_HPG
ATT_DH = 128
ATT_OUT = ATT_HPG * ATT_DH
ATT_BLOCK = 128
ROPE_THETA = 10000.0
FFN_CONV = 3

ROW_TILE = 512
PROJ_COLS = 512
SSD_CHUNK = 128
GLA_CHUNK = 128
ATT_ROWS = 512
SMALL_COLS = LANES

NEG_INF = float("-inf")


def _params(sem):
    return pltpu.CompilerParams(dimension_semantics=sem, vmem_limit_bytes=VMEM_LIMIT_BYTES)


def _const_spec(shape):
    zeros = (0,) * len(shape)
    return pl.BlockSpec(shape, lambda *_: zeros, pipeline_mode=pl.Buffered(1))


def _sigmoid_of_half(u):
    return 0.5 * jnp.tanh(u) + 0.5


def _silu_of_half(u):
    return u * jnp.tanh(u) + u


def _softplus(x):
    return jnp.maximum(x, 0.0) + jnp.log(1.0 + jnp.exp(-jnp.abs(x)))


def _dot(a, b):
    return jnp.dot(a, b, preferred_element_type=F32)


def _dot_nt(a, b):
    return lax.dot_general(a, b, (((1,), (1,)), ((), ())), preferred_element_type=F32)


def _dot_tn(a, b):
    return lax.dot_general(a, b, (((0,), (0,)), ((), ())), preferred_element_type=F32)


def _shift_rows(x, j, halo):
    n, c = x.shape
    x3 = x.reshape(n // SUBLANES, SUBLANES, c)
    prev = jnp.concatenate([halo[None], x3[:-1]], axis=0)
    sub = lax.broadcasted_iota(jnp.int32, x3.shape, 1)
    return pltpu.roll(jnp.where(sub >= SUBLANES - j, prev, x3), j, 1).reshape(n, c)


def _cumsum_rows(x):
    n = x.shape[0]
    row = lax.broadcasted_iota(jnp.int32, x.shape, 0)
    k = 1
    while k < n:
        x = x + jnp.where(row >= k, pltpu.roll(x, k, 0), 0.0)
        k *= 2
    return x


def _rmsnorm_rows(x, w):
    return x * lax.rsqrt(jnp.mean(x * x, axis=-1, keepdims=True) + RMS_EPS) * w


def _rope_table_kernel(inv_ref, cos_ref, sin_ref):
    rows = cos_ref.shape[0]
    base = pl.program_id(0) * rows
    pos = (base + lax.broadcasted_iota(jnp.int32, (rows, ATT_DH), 0)).astype(F32)
    lane = lax.broadcasted_iota(jnp.int32, (rows, ATT_DH), 1)
    ang = pos * inv_ref[...]
    cos_ref[...] = jnp.cos(ang)
    s = jnp.sin(ang)
    sin_ref[...] = jnp.where(lane < ATT_DH // 2, -s, s)


def _rope_table(seq):
    half = ATT_DH // 2
    inv = ROPE_THETA ** (-jnp.arange(half, dtype=F32) / half)
    inv = jnp.concatenate([inv, inv]).reshape(1, ATT_DH)
    rows = min(seq, 1024)
    return pl.pallas_call(
        _rope_table_kernel,
        grid=(seq // rows,),
        in_specs=[pl.BlockSpec((1, ATT_DH), lambda i: (0, 0))],
        out_specs=[pl.BlockSpec((rows, ATT_DH), lambda i: (i, 0))] * 2,
        out_shape=[jax.ShapeDtypeStruct((seq, ATT_DH), F32)] * 2,
        compiler_params=_params(("parallel",)),
        name="rope_table",
    )(inv)


def _norm_kernel(x_ref, w_ref, h_ref):
    h_ref[...] = _rmsnorm_rows(x_ref[...], w_ref[...]).astype(h_ref.dtype)


def _first_norm(x2d, w):
    t, d = x2d.shape
    return pl.pallas_call(
        _norm_kernel,
        grid=(t // ROW_TILE,),
        in_specs=[pl.BlockSpec((ROW_TILE, d), lambda i: (i, 0)), _const_spec((1, d))],
        out_specs=pl.BlockSpec((ROW_TILE, d), lambda i: (i, 0)),
        out_shape=jax.ShapeDtypeStruct((t, d), BF16),
        compiler_params=_params(("parallel",)),
        name="first_norm",
    )(x2d, w.reshape(1, d))


def _ssd_proj_kernel(h_ref, w_ref, cw_ref, cb_ref, xc_ref, zs_ref, sm_ref, halo_ref, *, tiles_per_seq):
    rows = h_ref.shape[0]

    @pl.when(pl.program_id(0) % tiles_per_seq == 0)
    def _():
        halo_ref[...] = jnp.zeros_like(halo_ref)

    h = h_ref[...]
    for j in range(SSD_CONV_DIM // PROJ_COLS):
        cols = slice(j * PROJ_COLS, (j + 1) * PROJ_COLS)
        acc = _dot(h, w_ref[:, cols])
        halo = halo_ref[:, cols]
        halo_ref[:, cols] = acc[rows - SUBLANES:, :]
        cw = cw_ref[:, cols]
        u = acc * cw[SSD_CONV - 1:SSD_CONV, :] + cb_ref[:, cols]
        for s in range(1, SSD_CONV):
            u = u + _shift_rows(acc, s, halo) * cw[SSD_CONV - 1 - s:SSD_CONV - s, :]
        xc_ref[:, cols] = _silu_of_half(u).astype(xc_ref.dtype)
    for j in range(SSD_INNER // PROJ_COLS):
        cols = slice(j * PROJ_COLS, (j + 1) * PROJ_COLS)
        wcols = slice(SSD_CONV_DIM + j * PROJ_COLS, SSD_CONV_DIM + (j + 1) * PROJ_COLS)
        zs_ref[:, cols] = _silu_of_half(_dot(h, w_ref[:, wcols])).astype(zs_ref.dtype)
    sm_ref[...] = _dot(h, w_ref[:, SSD_CONV_DIM + SSD_INNER:])


def _ssd_project(h, w, conv_w, conv_b, seq):
    t, d = h.shape
    row = lambda i: (i, 0)
    return pl.pallas_call(
        functools.partial(_ssd_proj_kernel, tiles_per_seq=seq // ROW_TILE),
        grid=(t // ROW_TILE,),
        in_specs=[pl.BlockSpec((ROW_TILE, d), row), _const_spec(w.shape),
                  _const_spec((SSD_CONV, SSD_CONV_DIM)), _const_spec((1, SSD_CONV_DIM))],
        out_specs=[pl.BlockSpec((ROW_TILE, SSD_CONV_DIM), row), pl.BlockSpec((ROW_TILE, SSD_INNER), row),
                   pl.BlockSpec((ROW_TILE, SMALL_COLS), row)],
        out_shape=[jax.ShapeDtypeStruct((t, SSD_CONV_DIM), BF16), jax.ShapeDtypeStruct((t, SSD_INNER), BF16),
                   jax.ShapeDtypeStruct((t, SMALL_COLS), F32)],
        scratch_shapes=[pltpu.VMEM((SUBLANES, SSD_CONV_DIM), F32)],
        compiler_params=_params(("arbitrary",)),
        name="proj_ssd",
    )(h, w, 0.5 * conv_w.astype(F32), 0.5 * conv_b.astype(F32).reshape(1, -1))


def _gla_gates_proj_kernel(h_ref, w_ref, gla_ref, gt_ref):
    h = h_ref[...]
    n_gla = gla_ref.shape[1]
    for j in range(w_ref.shape[1] // PROJ_COLS):
        c0 = j * PROJ_COLS
        acc = _dot(h, w_ref[:, c0:c0 + PROJ_COLS])
        if c0 >= n_gla:
            gt_ref[:, c0 - n_gla:c0 - n_gla + PROJ_COLS] = _sigmoid_of_half(acc).astype(gt_ref.dtype)
        else:
            if c0 >= 2 * GLA_KEY + GLA_VAL:
                acc = _silu_of_half(acc)
            gla_ref[:, c0:c0 + PROJ_COLS] = acc.astype(gla_ref.dtype)


def _gla_gates_project(h, w):
    t, d = h.shape
    n_gla = 2 * GLA_KEY + 2 * GLA_VAL
    n_gates = w.shape[1] - n_gla
    row = lambda i: (i, 0)
    return pl.pallas_call(
        _gla_gates_proj_kernel,
        grid=(t // ROW_TILE,),
        in_specs=[pl.BlockSpec((ROW_TILE, d), row), _const_spec(w.shape)],
        out_specs=[pl.BlockSpec((ROW_TILE, n_gla), row), pl.BlockSpec((ROW_TILE, n_gates), row)],
        out_shape=[jax.ShapeDtypeStruct((t, n_gla), BF16), jax.ShapeDtypeStruct((t, n_gates), BF16)],
        compiler_params=_params(("parallel",)),
        name="proj_gla_gates",
    )(h, w)


def _att_proj_kernel(h_ref, w_ref, cos_ref, sin_ref, *rest):
    out_refs, scr_ref = rest[:-1], rest[-1]
    h = h_ref[...]
    rows = h.shape[0]
    cos = cos_ref[...]
    sin = sin_ref[...]
    for kind in range(3):
        for g, d in enumerate(ATT_DILATIONS):
            c0 = (kind * ATT_GROUPS + g) * ATT_OUT
            acc = _dot(h, w_ref[:, c0:c0 + ATT_OUT])
            o_ref = out_refs[kind * ATT_GROUPS + g]
            for a in range(ATT_HPG):
                xh = acc[:, a * ATT_DH:(a + 1) * ATT_DH]
                if kind < 2:
                    xh = xh * cos + pltpu.roll(xh, ATT_DH // 2, 1) * sin
                if kind == 0:
                    xh = xh * (ATT_DH ** -0.5)
                if d == 1:
                    o_ref[:, a * ATT_DH:(a + 1) * ATT_DH] = xh.astype(o_ref.dtype)
                    continue
                scr_ref[a * rows:(a + 1) * rows, :] = xh
                for r in range(d):
                    c = r * ATT_OUT + a * ATT_DH
                    o_ref[:, c:c + ATT_DH] = (
                        scr_ref[pl.ds(a * rows + r, rows // d, stride=d), :].astype(o_ref.dtype))


def _att_project(h, w_qkv, rope, seq):
    t, dm = h.shape
    tiles_per_seq = seq // ROW_TILE
    rope_spec = pl.BlockSpec((ROW_TILE, ATT_DH), lambda i: (i % tiles_per_seq, 0))
    out_specs, out_shape = [], []
    for _ in range(3):
        for d in ATT_DILATIONS:
            out_specs.append(pl.BlockSpec((ROW_TILE // d, d * ATT_OUT), lambda i: (i, 0)))
            out_shape.append(jax.ShapeDtypeStruct((t // d, d * ATT_OUT), BF16))
    return pl.pallas_call(
        _att_proj_kernel,
        grid=(t // ROW_TILE,),
        in_specs=[pl.BlockSpec((ROW_TILE, dm), lambda i: (i, 0)), _const_spec(w_qkv.shape),
                  rope_spec, rope_spec],
        out_specs=out_specs,
        out_shape=out_shape,
        scratch_shapes=[pltpu.VMEM((ATT_HPG * ROW_TILE, ATT_DH), F32)],
        compiler_params=_params(("parallel",)),
        name="proj_att",
    )(h, w_qkv, *rope)


def _ssd_kernel(xc_ref, zs_ref, sm_ref, dtb_ref, aneg_ref, dsk_ref, nw_ref, y_ref, st_ref):
    L = xc_ref.shape[0]
    n_state = SSD_STATE
    pair = 2 * SSD_HEAD_DIM
    hpg = SSD_HEADS // SSD_GROUPS

    @pl.when(pl.program_id(1) == 0)
    def _():
        st_ref[...] = jnp.zeros_like(st_ref)

    dt = _softplus(sm_ref[...] + dtb_ref[...])
    a_cum = _cumsum_rows(dt * aneg_ref[...])
    a_last = a_cum[L - 1:L, :]
    w_end = jnp.exp(a_last - a_cum) * dt
    e_cum = jnp.exp(a_cum)
    chunk_decay = jnp.exp(a_last)
    a_cum_t = a_cum.T
    dt_t = dt.T

    ti = lax.broadcasted_iota(jnp.int32, (L, L), 0)
    si = lax.broadcasted_iota(jnp.int32, (L, L), 1)
    causal = ti >= si
    low = lax.broadcasted_iota(jnp.int32, (L, pair), 1) < SSD_HEAD_DIM
    low_row = lax.broadcasted_iota(jnp.int32, (1, pair), 1) < SSD_HEAD_DIM

    b_off = SSD_INNER
    c_off = SSD_INNER + SSD_GROUPS * n_state
    y_blocks = []
    for p in range(SSD_HEADS // 2):
        g = (2 * p) // hpg
        if (2 * p) % hpg == 0:
            b_g = xc_ref[:, b_off + g * n_state:b_off + (g + 1) * n_state]
            c_bf = xc_ref[:, c_off + g * n_state:c_off + (g + 1) * n_state]
            cb = _dot_nt(c_bf, b_g)
            c_g = c_bf.astype(F32)
        x_pair = xc_ref[:, p * pair:(p + 1) * pair]
        st = st_ref[p]
        lhs = []
        for h in (2 * p, 2 * p + 1):
            seg = a_cum[:, h:h + 1] - a_cum_t[h:h + 1, :]
            dec = jnp.exp(jnp.where(causal, seg, NEG_INF))
            m = (cb * dec * dt_t[h:h + 1, :]).astype(BF16)
            c_s = (c_g * e_cum[:, h:h + 1]).astype(BF16)
            lhs.append(jnp.concatenate([m, c_s], axis=1))
        yy = _dot(jnp.concatenate(lhs, axis=0), jnp.concatenate([x_pair, st.astype(BF16)], axis=0))
        x_f = x_pair.astype(F32)
        y_blocks.append(jnp.where(low, yy[:L], yy[L:]) + dsk_ref[:, p * pair:(p + 1) * pair] * x_f)

        w_pair = jnp.where(low, w_end[:, 2 * p:2 * p + 1], w_end[:, 2 * p + 1:2 * p + 2])
        cd = jnp.where(low_row, chunk_decay[:, 2 * p:2 * p + 1], chunk_decay[:, 2 * p + 1:2 * p + 2])
        st_ref[p] = st * cd + _dot_tn(b_g, (x_f * w_pair).astype(BF16))

    y = jnp.concatenate(y_blocks, axis=1)
    y = y * zs_ref[...].astype(F32)
    gw = SSD_INNER // SSD_GROUPS
    nw = nw_ref[...]
    for g in range(SSD_GROUPS):
        cols = slice(g * gw, (g + 1) * gw)
        y_ref[:, cols] = _rmsnorm_rows(y[:, cols], nw[:, cols]).astype(y_ref.dtype)


def _ssd_mixer(xc, zs, small, dt_bias, a_log, d_skip, norm_w, bsz, seq):
    L = SSD_CHUNK
    nc = seq // L
    pad = SMALL_COLS - SSD_HEADS
    dtb = jnp.pad(dt_bias.astype(F32), (0, pad)).reshape(1, SMALL_COLS)
    aneg = jnp.pad(-jnp.exp(a_log.astype(F32)), (0, pad)).reshape(1, SMALL_COLS)
    dsk = jnp.repeat(d_skip.astype(F32), SSD_HEAD_DIM).reshape(1, SSD_INNER)
    row = lambda b, c: (b * nc + c, 0)
    return pl.pallas_call(
        _ssd_kernel,
        grid=(bsz, nc),
        in_specs=[
            pl.BlockSpec((L, SSD_CONV_DIM), row),
            pl.BlockSpec((L, SSD_INNER), row),
            pl.BlockSpec((L, SMALL_COLS), row),
            _const_spec((1, SMALL_COLS)),
            _const_spec((1, SMALL_COLS)),
            _const_spec((1, SSD_INNER)),
            _const_spec((1, SSD_INNER)),
        ],
        out_specs=pl.BlockSpec((L, SSD_INNER), row),
        out_shape=jax.ShapeDtypeStruct((bsz * seq, SSD_INNER), BF16),
        scratch_shapes=[pltpu.VMEM((SSD_HEADS // 2, SSD_STATE, 2 * SSD_HEAD_DIM), F32)],
        compiler_params=_params(("parallel", "arbitrary")),
        name="ssd_mixer",
    )(xc, zs, small, dtb, aneg, dsk, norm_w.astype(F32).reshape(1, -1))


def _gla_level_ref_rows(b_ref, m, C):
    blk = 2 * m
    width = b_ref.shape[1]
    if blk >= SUBLANES:
        parts = []
        for s in range(0, C, blk):
            row = b_ref[s + m - 1:s + m, :]
            parts.append(jnp.broadcast_to(row, (blk, width)))
        return parts[0] if len(parts) == 1 else jnp.concatenate(parts, axis=0)
    b = b_ref[...]
    off = lax.broadcasted_iota(jnp.int32, (C, width), 0) % blk
    out = b
    for o in range(blk):
        delta = o - (m - 1)
        if delta == 0:
            continue
        out = jnp.where(off == o, pltpu.roll(b, delta % C, 0), out)
    return out


def _gla_kernel(qk_ref, v_ref, rs_ref, sm_ref, gw_ref, gb_ref, nw_ref, y_ref, st_ref, b_ref):
    C = qk_ref.shape[0]
    pair = 2 * GLA_DK

    @pl.when(pl.program_id(1) == 0)
    def _():
        st_ref[...] = jnp.zeros_like(st_ref)

    qk = qk_ref[...].astype(F32)
    q = qk[:, :GLA_KEY]
    k = qk[:, GLA_KEY:]

    s_f = sm_ref[...]
    s_hi = s_f.astype(BF16)
    s_lo = (s_f - s_hi.astype(F32)).astype(BF16)
    g_f = gw_ref[...]
    g_hi = g_f.astype(BF16)
    g_lo = (g_f - g_hi.astype(F32)).astype(BF16)
    pre = _dot(s_hi, g_hi) + _dot(s_hi, g_lo) + _dot(s_lo, g_hi) + gb_ref[...]
    log_a = -_softplus(-pre) * (1.0 / GLA_TAU)
    b = _cumsum_rows(log_a)
    b_ref[...] = b
    b_last = b[C - 1:C, :]

    rowi = lax.broadcasted_iota(jnp.int32, (C, GLA_KEY), 0)
    lane = lax.broadcasted_iota(jnp.int32, (C, pair), 1)
    head_lanes = (lane < GLA_DK, lane >= GLA_DK)
    ti = lax.broadcasted_iota(jnp.int32, (C, C), 0)
    si = lax.broadcasted_iota(jnp.int32, (C, C), 1)
    ts_xor = ti ^ si

    head_mask = [jnp.where(sel, 1.0, 0.0).astype(BF16) for sel in head_lanes]

    def head_ops(qa, ka, h, mask_k=False):
        cols = slice((h // 2) * pair, (h // 2 + 1) * pair)
        qh = qa[:, cols] * head_mask[h % 2]
        kh = ka[:, cols] * head_mask[h % 2] if mask_k else ka[:, cols]
        return qh, kh

    scores = []
    q_bf, k_bf = q.astype(BF16), k.astype(BF16)
    for h in range(GLA_HEADS):
        qh, kh = head_ops(q_bf, k_bf, h)
        scores.append(jnp.where(ti == si, _dot_nt(qh, kh), 0.0))
    m = C // 2
    while m >= 1:
        blk = 2 * m
        upper = (rowi & m) != 0
        beta = _gla_level_ref_rows(b_ref, m, C)
        e = jnp.exp(jnp.where(upper, b - beta, beta - b))
        qt = jnp.where(upper, q * e, 0.0).astype(BF16)
        kt = jnp.where(upper, 0.0, k * e).astype(BF16)
        for h in range(GLA_HEADS):
            qh, kh = head_ops(qt, kt, h)
            p = _dot_nt(qh, kh)
            if blk < C:
                p = jnp.where(ts_xor < blk, p, 0.0)
            scores[h] = scores[h] + p
        m //= 2

    qe = (q * jnp.exp(b)).astype(BF16)
    k_end = (k * jnp.exp(b_last - b)).astype(BF16)
    st_decay = jnp.exp(b_last)
    nw = nw_ref[...]
    for h in range(GLA_HEADS):
        j = h // 2
        vh = v_ref[:, h * GLA_DV:(h + 1) * GLA_DV]
        st = st_ref[h]
        qh, kh = head_ops(qe, k_end, h, mask_k=True)
        o = _dot(scores[h].astype(BF16), vh) + _dot_nt(qh, st.astype(BF16))
        st_ref[h] = st * st_decay[:, j * pair:(j + 1) * pair] + _dot_tn(vh, kh)
        o = _rmsnorm_rows(o, nw) * rs_ref[:, h * GLA_DV:(h + 1) * GLA_DV].astype(F32)
        y_ref[:, h * GLA_DV:(h + 1) * GLA_DV] = o.astype(y_ref.dtype)


def _gla_mixer(gla_in, small, gate_w, gate_b, norm_w, bsz, seq):
    C = GLA_CHUNK
    nc = seq // C
    gw = jnp.zeros((SMALL_COLS, GLA_KEY), F32).at[SSD_HEADS:SSD_HEADS + GLA_RANK].set(gate_w.astype(F32))
    row = lambda col: (lambda b, c: (b * nc + c, col))
    return pl.pallas_call(
        _gla_kernel,
        grid=(bsz, nc),
        in_specs=[
            pl.BlockSpec((C, 2 * GLA_KEY), row(0)),
            pl.BlockSpec((C, GLA_VAL), row(1)),
            pl.BlockSpec((C, GLA_VAL), row(2)),
            pl.BlockSpec((C, SMALL_COLS), row(0)),
            _const_spec((SMALL_COLS, GLA_KEY)),
            _const_spec((1, GLA_KEY)),
            _const_spec((1, GLA_DV)),
        ],
        out_specs=pl.BlockSpec((C, GLA_VAL), row(0)),
        out_shape=jax.ShapeDtypeStruct((bsz * seq, GLA_VAL), BF16),
        scratch_shapes=[
            pltpu.VMEM((GLA_HEADS, GLA_DV, 2 * GLA_DK), F32),
            pltpu.VMEM((C, GLA_KEY), F32),
        ],
        compiler_params=_params(("parallel", "arbitrary")),
        name="gla_mixer",
    )(gla_in, gla_in, gla_in, small, gw, gate_b.astype(F32).reshape(1, -1),
      norm_w.astype(F32).reshape(1, -1))


def _att_kernel(q_ref, kh_ref, k_ref, vh_ref, v_ref, o_ref, lse_ref, kbuf_ref, vbuf_ref):
    R = q_ref.shape[0]
    blk = ATT_BLOCK
    kbuf_ref[:blk, :] = kh_ref[...]
    kbuf_ref[blk:, :] = k_ref[...]
    vbuf_ref[:blk, :] = vh_ref[...]
    vbuf_ref[blk:, :] = v_ref[...]
    qi = lax.broadcasted_iota(jnp.int32, (blk, 2 * blk), 0)
    kj = lax.broadcasted_iota(jnp.int32, (blk, 2 * blk), 1)
    band = jnp.abs(2 * (kj - qi) - blk) <= blk
    first_key = jnp.where(pl.program_id(2) > 0, 0, blk)
    lane = lax.broadcasted_iota(jnp.int32, (blk, LANES), 1)
    for sb in range(R // blk):
        rows = slice(sb * blk, (sb + 1) * blk)
        win = slice(sb * blk, (sb + 2) * blk)
        lse_tile = jnp.zeros((blk, LANES), F32)
        for h in range(ATT_HPG):
            cols = slice(h * ATT_DH, (h + 1) * ATT_DH)
            s = _dot_nt(q_ref[rows, cols], kbuf_ref[win, cols])
            s = jnp.where(band, s, NEG_INF)
            if sb == 0:
                s = jnp.where(kj >= first_key, s, NEG_INF)
            mx = jnp.max(s, axis=-1, keepdims=True)
            p = jnp.exp(s - mx)
            den = jnp.sum(p, axis=-1, keepdims=True)
            o = _dot(p.astype(BF16), vbuf_ref[win, cols])
            o_ref[rows, cols] = (o * (1.0 / den)).astype(o_ref.dtype)
            lse_tile = jnp.where(lane == h, mx + jnp.log(den), lse_tile)
        lse_ref[rows, :] = lse_tile


def _dilated_attention(q, k, v, g, bsz, seq):
    d = ATT_DILATIONS[g]
    assert ATT_WINDOWS[g] // d == ATT_BLOCK and seq % (d * ATT_BLOCK) == 0
    sub = seq // d
    R = min(ATT_ROWS, sub)
    nb = sub // R
    per = R // ATT_BLOCK
    main = pl.BlockSpec((R, ATT_OUT), lambda b, r, i: (b * nb + i, r))
    halo = pl.BlockSpec((ATT_BLOCK, ATT_OUT), lambda b, r, i: (jnp.maximum((b * nb + i) * per - 1, 0), r))
    return pl.pallas_call(
        _att_kernel,
        grid=(bsz, d, nb),
        in_specs=[main, halo, main, halo, main],
        out_specs=[main, pl.BlockSpec((R, LANES), lambda b, r, i: (b * nb + i, r))],
        out_shape=[
            jax.ShapeDtypeStruct((bsz * sub, d * ATT_OUT), BF16),
            jax.ShapeDtypeStruct((bsz * sub, d * LANES), F32),
        ],
        scratch_shapes=[pltpu.VMEM((R + ATT_BLOCK, ATT_OUT), BF16)] * 2,
        compiler_params=_params(("parallel", "parallel", "arbitrary")),
        name=f"dilated_attention_d{d}",
    )(q, k, k, v, v)


def _interleave_rows(src_ref, scr_ref, d, rows):
    n_blk = scr_ref.shape[0] // rows
    if d == 1:
        return [src_ref[:, a * LANES:(a + 1) * LANES].astype(F32) for a in range(n_blk)]
    for r in range(d):
        for a in range(n_blk):
            c = (r * n_blk + a) * LANES
            scr_ref[pl.ds(a * rows + r, rows // d, stride=d), :] = src_ref[:, c:c + LANES].astype(F32)
    return [scr_ref[a * rows:(a + 1) * rows, :] for a in range(n_blk)]


def _merge_kernel(x_ref, ys_ref, yg_ref, o0_ref, o1_ref, o2_ref, l0_ref, l1_ref, l2_ref, gt_ref,
                  wb_ref, wo_ref, nw_ref, xo_ref, h_ref, oscr_ref, lscr_ref):
    rows = x_ref.shape[0]
    d = x_ref.shape[1]
    lses = [_interleave_rows(l_ref, lscr_ref, dil, rows)[0]
            for l_ref, dil in zip((l0_ref, l1_ref, l2_ref), ATT_DILATIONS)]
    mx = jnp.maximum(jnp.maximum(lses[0], lses[1]), lses[2])
    es = [jnp.exp(l - mx) for l in lses]
    inv = 1.0 / (es[0] + es[1] + es[2])
    acc = [jnp.zeros((rows, ATT_DH), F32) for _ in range(ATT_HPG)]
    for g, o_ref in enumerate((o0_ref, o1_ref, o2_ref)):
        o_g = _interleave_rows(o_ref, oscr_ref, ATT_DILATIONS[g], rows)
        w_g = es[g] * inv
        for h in range(ATT_HPG):
            acc[h] = acc[h] + w_g[:, h:h + 1] * o_g[h]
    y_att = jnp.concatenate([a.astype(BF16) for a in acc], axis=1)

    merged = _sig_gate(gt_ref, 0, d) * _dot(ys_ref[...], wb_ref[:SSD_INNER, :])
    merged = merged + _sig_gate(gt_ref, 1, d) * _dot(yg_ref[...], wb_ref[SSD_INNER:SSD_INNER + GLA_VAL, :])
    merged = merged + _sig_gate(gt_ref, 2, d) * _dot(y_att, wb_ref[SSD_INNER + GLA_VAL:, :])
    x_new = x_ref[...] + _dot(merged.astype(BF16), wo_ref[...])
    xo_ref[...] = x_new
    h_ref[...] = _rmsnorm_rows(x_new, nw_ref[...]).astype(h_ref.dtype)


def _sig_gate(gt_ref, i, d):
    return gt_ref[:, i * d:(i + 1) * d].astype(F32)


def _merge(x2d, y_ssd, y_gla, att_o, att_lse, gates, w_branch, w_out, norm_w):
    t, d = x2d.shape
    row = lambda i: (i, 0)
    tile = lambda n: pl.BlockSpec((ROW_TILE, n), row)
    dil_tile = lambda n: [pl.BlockSpec((ROW_TILE // dil, dil * n), row) for dil in ATT_DILATIONS]
    return pl.pallas_call(
        _merge_kernel,
        grid=(t // ROW_TILE,),
        in_specs=[tile(d), tile(SSD_INNER), tile(GLA_VAL)] + dil_tile(ATT_OUT) + dil_tile(LANES)
        + [tile(3 * d), _const_spec(w_branch.shape), _const_spec(w_out.shape), _const_spec((1, d))],
        out_specs=[tile(d), tile(d)],
        out_shape=[jax.ShapeDtypeStruct((t, d), F32), jax.ShapeDtypeStruct((t, d), BF16)],
        scratch_shapes=[pltpu.VMEM((ATT_HPG * ROW_TILE, ATT_DH), F32), pltpu.VMEM((ROW_TILE, LANES), F32)],
        compiler_params=_params(("parallel",)),
        name="merge_out",
    )(x2d, y_ssd, y_gla, *att_o, *att_lse, gates, w_branch, w_out, norm_w.astype(F32).reshape(1, d))


def _ffn_kernel(x_ref, h_ref, wu_ref, cw_ref, cb_ref, wd_ref, nw_ref, xo_ref, ho_ref,
                halo_ref, *, tiles_per_seq):
    rows = x_ref.shape[0]
    hid = wd_ref.shape[0]

    @pl.when(pl.program_id(0) % tiles_per_seq == 0)
    def _():
        halo_ref[...] = jnp.zeros_like(halo_ref)

    h = h_ref[...]
    gate = _dot(h, wu_ref[:, :hid])
    val = _dot(h, wu_ref[:, hid:])
    halo = halo_ref[...]
    halo_ref[...] = gate[rows - SUBLANES:, :]
    cw = cw_ref[...]
    u = gate * cw[FFN_CONV - 1:FFN_CONV, :] + cb_ref[...]
    for j in range(1, FFN_CONV):
        u = u + _shift_rows(gate, j, halo) * cw[FFN_CONV - 1 - j:FFN_CONV - j, :]
    act = (_silu_of_half(u) * val).astype(BF16)
    x_new = x_ref[...] + _dot(act, wd_ref[...])
    xo_ref[...] = x_new
    ho_ref[...] = _rmsnorm_rows(x_new, nw_ref[...]).astype(ho_ref.dtype)


def _conv_ffn(x2d, h, w_up, conv_w, conv_b, w_down, norm_w, seq, out_dtype):
    t, d = x2d.shape
    hid = w_down.shape[0]
    row = lambda i: (i, 0)
    tile = pl.BlockSpec((ROW_TILE, d), row)
    return pl.pallas_call(
        functools.partial(_ffn_kernel, tiles_per_seq=seq // ROW_TILE),
        grid=(t // ROW_TILE,),
        in_specs=[tile, tile, _const_spec((d, 2 * hid)), _const_spec((FFN_CONV, hid)),
                  _const_spec((1, hid)), _const_spec((hid, d)), _const_spec((1, d))],
        out_specs=[tile, tile],
        out_shape=[jax.ShapeDtypeStruct((t, d), F32), jax.ShapeDtypeStruct((t, d), out_dtype)],
        scratch_shapes=[pltpu.VMEM((SUBLANES, hid), F32)],
        compiler_params=_params(("arbitrary",)),
        name="conv_ffn",
    )(x2d, h, w_up, 0.5 * conv_w.astype(F32), 0.5 * conv_b.astype(F32).reshape(1, hid), w_down,
      norm_w.astype(F32).reshape(1, d))


def _split_in_proj(w_in_l):
    widths = (SSD_INNER, SSD_CONV_DIM, SSD_HEADS, GLA_KEY, GLA_KEY, GLA_VAL, GLA_RANK, GLA_VAL,
              3 * ATT_HEADS * ATT_DH, None)
    pieces = []
    start = 0
    for wd in widths:
        stop = w_in_l.shape[1] if wd is None else start + wd
        pieces.append(w_in_l[:, start:stop])
        start = stop
    z, xbc, dt, gq, gk, gv, glr, gr, qkv, gates = pieces
    d = w_in_l.shape[0]
    pad = jnp.zeros((d, SMALL_COLS - SSD_HEADS - GLA_RANK), w_in_l.dtype)
    return {
        "ssd": jnp.concatenate([xbc, 0.5 * z, dt, glr, pad], axis=1),
        "gla_gates": jnp.concatenate([gq * (GLA_DK ** -0.5), gk, gv, 0.5 * gr, 0.5 * gates], axis=1),
        "att": qkv,
    }


def kernel(x, norm1_w, w_in, ssd_conv_w, ssd_conv_b, ssd_dt_bias, ssd_a_log, ssd_d, ssd_norm_w,
           gla_gate_w, gla_gate_b, gla_norm_w, w_branch, w_out, norm2_w, ffn_up, ffn_conv_w,
           ffn_conv_b, ffn_down, final_norm_w):
    bsz, seq, d = x.shape
    depth = w_in.shape[0]
    assert seq % ROW_TILE == 0 and seq % SSD_CHUNK == 0 and seq % GLA_CHUNK == 0
    rope = _rope_table(seq)
    x2d = x.reshape(bsz * seq, d)
    h = _first_norm(x2d, norm1_w[0].astype(F32))
    for l in range(depth):
        w = _split_in_proj(w_in[l].astype(BF16))
        xc, zs, small = _ssd_project(h, w["ssd"], ssd_conv_w[l], ssd_conv_b[l], seq)
        gla_in, gates = _gla_gates_project(h, w["gla_gates"])
        att_in = _att_project(h, w["att"], rope, seq)

        y_ssd = _ssd_mixer(xc, zs, small, ssd_dt_bias[l], ssd_a_log[l], ssd_d[l], ssd_norm_w[l], bsz, seq)
        y_gla = _gla_mixer(gla_in, small, gla_gate_w[l], gla_gate_b[l], gla_norm_w[l], bsz, seq)
        att = [_dilated_attention(att_in[g], att_in[ATT_GROUPS + g], att_in[2 * ATT_GROUPS + g], g, bsz, seq)
               for g in range(ATT_GROUPS)]
        x2d, h = _merge(x2d, y_ssd, y_gla, [a[0] for a in att], [a[1] for a in att], gates,
                        w_branch[l].astype(BF16), w_out[l].astype(BF16), norm2_w[l])

        last = l == depth - 1
        next_w = final_norm_w if last else norm1_w[l + 1]
        x2d, h = _conv_ffn(x2d, h, ffn_up[l].astype(BF16), ffn_conv_w[l], ffn_conv_b[l],
                           ffn_down[l].astype(BF16), next_w, seq, F32 if last else BF16)
    return h.reshape(bsz, seq, d)
```

```python
import functools
from typing import Callable, NamedTuple

import jax
import jax.numpy as jnp
import numpy as np
from jax import lax
from jax.experimental import pallas as pl
from jax.experimental.pallas import tpu as pltpu

F32 = jnp.float32
BF16 = jnp.bfloat16

LANES = 128
SUBLANES = 8
VMEM_LIMIT_BYTES = 56 * 1024 * 1024

RMS_EPS = 1e-6
SSD_HEADS = 16
SSD_HEAD_DIM = 64
SSD_INNER = SSD_HEADS * SSD_HEAD_DIM
SSD_STATE = 128
SSD_GROUPS = 4
SSD_CONV = 4
SSD_CONV_DIM = SSD_INNER + 2 * SSD_GROUPS * SSD_STATE
GLA_HEADS = 4
GLA_DK = 64
GLA_DV = 128
GLA_KEY = GLA_HEADS * GLA_DK
GLA_VAL = GLA_HEADS * GLA_DV
GLA_RANK = 16
GLA_TAU = 16.0
ATT_DILATIONS = (1, 4, 16)
ATT_WINDOWS = (128, 512, 2048)
ATT_GROUPS = 3
ATT_HPG = 4
ATT_HEADS = ATT_GROUPS * ATT_HPG
ATT_DH = 128
ATT_OUT = ATT_HPG * ATT_DH
ATT_BLOCK = 128
ROPE_THETA = 10000.0
FFN_CONV = 3

ROW_TILE = 512
PROJ_COLS = 512
SSD_CHUNK = 128
GLA_CHUNK = 128
SMALL_COLS = LANES
W_BLOCK = 3 * ATT_HEADS * ATT_DH
W_GLA_GATES_BLOCK, W_ATT_BLOCK, W_SSD_BLOCK = 0, 1, 2

NEG_INF = float("-inf")


def _params(sem):
    return pltpu.CompilerParams(dimension_semantics=sem, vmem_limit_bytes=VMEM_LIMIT_BYTES)


def _const_spec(shape, col_block=0):
    index = (0,) * (len(shape) - 1) + (col_block,)
    return pl.BlockSpec(shape, lambda *_: index, pipeline_mode=pl.Buffered(1))


class _Call(NamedTuple):
    kind: str
    name: str
    body: Callable
    args: tuple
    in_specs: tuple
    out_specs: tuple
    out_shape: tuple
    scratch_shapes: tuple = ()
    zero_scratch: bool = False


def _run(calls, steps):
    n_in = [len(c.args) for c in calls]
    n_out = [len(c.out_shape) for c in calls]
    n_scr = [len(c.scratch_shapes) for c in calls]

    def body(*refs):
        ins = refs[:sum(n_in)]
        outs = refs[sum(n_in):sum(n_in) + sum(n_out)]
        scrs = refs[sum(n_in) + sum(n_out):]

        @pl.when(pl.program_id(0) == 0)
        def _():
            s = 0
            for c, ns in zip(calls, n_scr):
                if c.zero_scratch:
                    for ref in scrs[s:s + ns]:
                        ref[...] = jnp.zeros_like(ref)
                s += ns

        i = o = s = 0
        for c, ni, no, ns in zip(calls, n_in, n_out, n_scr):
            c.body(*ins[i:i + ni], *outs[o:o + no], *scrs[s:s + ns])
            i, o, s = i + ni, o + no, s + ns

    flat = pl.pallas_call(
        body,
        grid=(steps,),
        in_specs=[sp for c in calls for sp in c.in_specs],
        out_specs=[sp for c in calls for sp in c.out_specs],
        out_shape=[sh for c in calls for sh in c.out_shape],
        scratch_shapes=[sc for c in calls for sc in c.scratch_shapes],
        compiler_params=_params(("arbitrary",)),
        name="__".join(c.name for c in calls),
    )(*[a for c in calls for a in c.args])
    outs, o = [], 0
    for no in n_out:
        outs.append(list(flat[o:o + no]))
        o += no
    return outs


def _interleave(streams, steps):
    pending = [next(st) for st in streams]
    done = [None] * len(streams)

    def advance(i, outs):
        try:
            pending[i] = streams[i].send(outs)
        except StopIteration as stop:
            pending[i], done[i] = None, stop.value

    while any(p is not None for p in pending):
        live = [i for i, p in enumerate(pending) if p is not None]
        group = live[:1]
        for i in live[1:]:
            if pending[i].kind != pending[group[0]].kind:
                group.append(i)
                break
        outs = _run([pending[i] for i in group], steps)
        for i, o in zip(group, outs):
            advance(i, o)
    return done


def _sigmoid_of_half(u):
    return 0.5 * jnp.tanh(u) + 0.5


def _silu_of_half(u):
    return u * jnp.tanh(u) + u


def _softplus(x):
    return jnp.maximum(x, 0.0) + jnp.log(1.0 + jnp.exp(-jnp.abs(x)))


def _dot(a, b):
    return jnp.dot(a, b, preferred_element_type=F32)


def _dot_nt(a, b):
    return lax.dot_general(a, b, (((1,), (1,)), ((), ())), preferred_element_type=F32)


def _dot_tn(a, b):
    return lax.dot_general(a, b, (((0,), (0,)), ((), ())), preferred_element_type=F32)


def _shift_rows(x, j, halo):
    n, c = x.shape
    x3 = x.reshape(n // SUBLANES, SUBLANES, c)
    prev = jnp.concatenate([halo[None], x3[:-1]], axis=0)
    sub = lax.broadcasted_iota(jnp.int32, x3.shape, 1)
    return pltpu.roll(jnp.where(sub >= SUBLANES - j, prev, x3), j, 1).reshape(n, c)


def _cumsum_rows(x):
    n = x.shape[0]
    row = lax.broadcasted_iota(jnp.int32, x.shape, 0)
    k = 1
    while k < n:
        x = x + jnp.where(row >= k, pltpu.roll(x, k, 0), 0.0)
        k *= 2
    return x


def _rmsnorm_rows(x, w):
    return x * lax.rsqrt(jnp.mean(x * x, axis=-1, keepdims=True) + RMS_EPS) * w


def _rope_table_kernel(inv_ref, cos_ref, sin_ref):
    rows = cos_ref.shape[0]
    base = pl.program_id(0) * rows
    pos = (base + lax.broadcasted_iota(jnp.int32, (rows, ATT_DH), 0)).astype(F32)
    lane = lax.broadcasted_iota(jnp.int32, (rows, ATT_DH), 1)
    ang = pos * inv_ref[...]
    cos_ref[...] = jnp.cos(ang)
    s = jnp.sin(ang)
    sin_ref[...] = jnp.where(lane < ATT_DH // 2, -s, s)


def _rope_table(seq):
    half = ATT_DH // 2
    inv = ROPE_THETA ** (-jnp.arange(half, dtype=F32) / half)
    inv = jnp.concatenate([inv, inv]).reshape(1, ATT_DH)
    rows = min(seq, 1024)
    return pl.pallas_call(
        _rope_table_kernel,
        grid=(seq // rows,),
        in_specs=[pl.BlockSpec((1, ATT_DH), lambda i: (0, 0))],
        out_specs=[pl.BlockSpec((rows, ATT_DH), lambda i: (i, 0))] * 2,
        out_shape=[jax.ShapeDtypeStruct((seq, ATT_DH), F32)] * 2,
        compiler_params=_params(("parallel",)),
        name="rope_table",
    )(inv)


def _norm_kernel(x_ref, w_ref, h_ref):
    h_ref[...] = _rmsnorm_rows(x_ref[...], w_ref[...]).astype(h_ref.dtype)


def _first_norm(x2d, w, tile0, seq):
    d = x2d.shape[1]
    return _Call("mxu", "first_norm", _norm_kernel, (x2d, w.astype(F32).reshape(1, d)),
                 (pl.BlockSpec((ROW_TILE, d), lambda i: (i + tile0, 0)), _const_spec((1, d))),
                 (pl.BlockSpec((ROW_TILE, d), lambda i: (i, 0)),), (jax.ShapeDtypeStruct((seq, d), BF16),))


def _ssd_proj_kernel(h_ref, w_ref, cw_ref, cb_ref, xc_ref, zs_ref, sm_ref, halo_ref):
    rows = h_ref.shape[0]

    h = h_ref[...]
    for j in range(SSD_CONV_DIM // PROJ_COLS):
        cols = slice(j * PROJ_COLS, (j + 1) * PROJ_COLS)
        acc = _dot(h, w_ref[:, cols])
        halo = halo_ref[:, cols]
        halo_ref[:, cols] = acc[rows - SUBLANES:, :]
        cw = cw_ref[:, cols]
        u = acc * cw[SSD_CONV - 1:SSD_CONV, :] + cb_ref[:, cols]
        for s in range(1, SSD_CONV):
            u = u + _shift_rows(acc, s, halo) * cw[SSD_CONV - 1 - s:SSD_CONV - s, :]
        xc_ref[:, cols] = _silu_of_half(u).astype(xc_ref.dtype)
    for j in range(SSD_INNER // PROJ_COLS):
        cols = slice(j * PROJ_COLS, (j + 1) * PROJ_COLS)
        wcols = slice(SSD_CONV_DIM + j * PROJ_COLS, SSD_CONV_DIM + (j + 1) * PROJ_COLS)
        zs_ref[:, cols] = _silu_of_half(_dot(h, w_ref[:, wcols])).astype(zs_ref.dtype)
    sm_ref[...] = _dot(h, w_ref[:, SSD_CONV_DIM + SSD_INNER:SSD_CONV_DIM + SSD_INNER + SMALL_COLS])


def _ssd_project(h, w, conv_w, conv_b):
    t, d = h.shape
    row = lambda i: (i, 0)
    return _Call(
        "mxu", "proj_ssd", _ssd_proj_kernel,
        (h, w, 0.5 * conv_w.astype(F32), 0.5 * conv_b.astype(F32).reshape(1, -1)),
        (pl.BlockSpec((ROW_TILE, d), row), _const_spec((d, W_BLOCK), W_SSD_BLOCK),
         _const_spec((SSD_CONV, SSD_CONV_DIM)), _const_spec((1, SSD_CONV_DIM))),
        (pl.BlockSpec((ROW_TILE, SSD_CONV_DIM), row), pl.BlockSpec((ROW_TILE, SSD_INNER), row),
         pl.BlockSpec((ROW_TILE, SMALL_COLS), row)),
        (jax.ShapeDtypeStruct((t, SSD_CONV_DIM), BF16), jax.ShapeDtypeStruct((t, SSD_INNER), BF16),
         jax.ShapeDtypeStruct((t, SMALL_COLS), F32)),
        (pltpu.VMEM((SUBLANES, SSD_CONV_DIM), F32),), zero_scratch=True)


def _gla_gates_proj_kernel(h_ref, w_ref, gla_ref, gt_ref):
    h = h_ref[...]
    n_gla = gla_ref.shape[1]
    for j in range(w_ref.shape[1] // PROJ_COLS):
        c0 = j * PROJ_COLS
        acc = _dot(h, w_ref[:, c0:c0 + PROJ_COLS])
        if c0 >= n_gla:
            gt_ref[:, c0 - n_gla:c0 - n_gla + PROJ_COLS] = _sigmoid_of_half(acc).astype(gt_ref.dtype)
        else:
            if c0 >= 2 * GLA_KEY + GLA_VAL:
                acc = _silu_of_half(acc)
            gla_ref[:, c0:c0 + PROJ_COLS] = acc.astype(gla_ref.dtype)


def _gla_gates_project(h, w):
    t, d = h.shape
    n_gla = 2 * GLA_KEY + 2 * GLA_VAL
    n_gates = W_BLOCK - n_gla
    row = lambda i: (i, 0)
    return _Call(
        "mxu", "proj_gla_gates", _gla_gates_proj_kernel, (h, w),
        (pl.BlockSpec((ROW_TILE, d), row), _const_spec((d, W_BLOCK), W_GLA_GATES_BLOCK)),
        (pl.BlockSpec((ROW_TILE, n_gla), row), pl.BlockSpec((ROW_TILE, n_gates), row)),
        (jax.ShapeDtypeStruct((t, n_gla), BF16), jax.ShapeDtypeStruct((t, n_gates), BF16)))


def _att_proj_kernel(h_ref, w_ref, cos_ref, sin_ref, *rest):
    out_refs, scr_ref = rest[:-1], rest[-1]
    h = h_ref[...]
    rows = h.shape[0]
    cos = cos_ref[...]
    sin = sin_ref[...]
    slot = 0
    for kind in range(3):
        for g, d in enumerate(ATT_DILATIONS):
            c0 = (kind * ATT_GROUPS + g) * ATT_OUT
            acc = _dot(h, w_ref[:, c0:c0 + ATT_OUT])
            o_ref = out_refs[kind * ATT_GROUPS + g]
            for a in range(ATT_HPG):
                xh = acc[:, a * ATT_DH:(a + 1) * ATT_DH]
                if kind < 2:
                    xh = xh * cos + pltpu.roll(xh, ATT_DH // 2, 1) * sin
                if kind == 0:
                    xh = xh * (ATT_DH ** -0.5)
                if d == 1:
                    o_ref[:, a * ATT_DH:(a + 1) * ATT_DH] = xh.astype(o_ref.dtype)
                    continue
                scr_ref[slot * rows:(slot + 1) * rows, :] = xh
                for r in range(d):
                    c = r * ATT_OUT + a * ATT_DH
                    o_ref[:, c:c + ATT_DH] = (
                        scr_ref[pl.ds(slot * rows + r, rows // d, stride=d), :].astype(o_ref.dtype))
                slot += 1


def _att_project(h, w_qkv, rope):
    t, dm = h.shape
    row = lambda i: (i, 0)
    rope_spec = pl.BlockSpec((ROW_TILE, ATT_DH), row)
    out_specs, out_shape = [], []
    for _ in range(3):
        for d in ATT_DILATIONS:
            out_specs.append(pl.BlockSpec((ROW_TILE // d, d * ATT_OUT), row))
            out_shape.append(jax.ShapeDtypeStruct((t // d, d * ATT_OUT), BF16))
    n_slots = 3 * sum(d > 1 for d in ATT_DILATIONS) * ATT_HPG
    return _Call(
        "mxu", "proj_att", _att_proj_kernel, (h, w_qkv, *rope),
        (pl.BlockSpec((ROW_TILE, dm), row), _const_spec((dm, W_BLOCK), W_ATT_BLOCK), rope_spec, rope_spec),
        tuple(out_specs), tuple(out_shape),
        (pltpu.VMEM((n_slots * ROW_TILE, ATT_DH), F32),))


def _ssd_kernel(xc_ref, zs_ref, sm_ref, dtb_ref, aneg_ref, dsk_ref, nw_ref, y_ref, st_ref):
    L = SSD_CHUNK
    pair = 2 * SSD_HEAD_DIM

    ti = lax.broadcasted_iota(jnp.int32, (L, L), 0)
    si = lax.broadcasted_iota(jnp.int32, (L, L), 1)
    causal = ti >= si
    low = lax.broadcasted_iota(jnp.int32, (L, pair), 1) < SSD_HEAD_DIM
    low_row = lax.broadcasted_iota(jnp.int32, (1, pair), 1) < SSD_HEAD_DIM
    for sub in range(xc_ref.shape[0] // L):
        rows = slice(sub * L, (sub + 1) * L)
        _ssd_chunk(xc_ref, zs_ref, sm_ref, dtb_ref, aneg_ref, dsk_ref, nw_ref, y_ref, st_ref, rows,
                   causal, low, low_row)


def _ssd_chunk(xc_ref, zs_ref, sm_ref, dtb_ref, aneg_ref, dsk_ref, nw_ref, y_ref, st_ref, rows,
               causal, low, low_row):
    L = SSD_CHUNK
    n_state = SSD_STATE
    pair = 2 * SSD_HEAD_DIM
    hpg = SSD_HEADS // SSD_GROUPS
    dt = _softplus(sm_ref[rows, :] + dtb_ref[...])
    a_cum = _cumsum_rows(dt * aneg_ref[...])
    a_last = a_cum[L - 1:L, :]
    w_end = jnp.exp(a_last - a_cum) * dt
    e_cum = jnp.exp(a_cum)
    chunk_decay = jnp.exp(a_last)
    a_cum_t = a_cum.T
    dt_t = dt.T

    b_off = SSD_INNER
    c_off = SSD_INNER + SSD_GROUPS * n_state
    y_blocks = []
    for p in range(SSD_HEADS // 2):
        g = (2 * p) // hpg
        if (2 * p) % hpg == 0:
            b_g = xc_ref[rows, b_off + g * n_state:b_off + (g + 1) * n_state]
            c_bf = xc_ref[rows, c_off + g * n_state:c_off + (g + 1) * n_state]
            cb = _dot_nt(c_bf, b_g)
            c_g = c_bf.astype(F32)
        x_pair = xc_ref[rows, p * pair:(p + 1) * pair]
        st = st_ref[p]
        lhs = []
        for h in (2 * p, 2 * p + 1):
            seg = a_cum[:, h:h + 1] - a_cum_t[h:h + 1, :]
            dec = jnp.exp(jnp.where(causal, seg, NEG_INF))
            m = (cb * dec * dt_t[h:h + 1, :]).astype(BF16)
            c_s = (c_g * e_cum[:, h:h + 1]).astype(BF16)
            lhs.append(jnp.concatenate([m, c_s], axis=1))
        yy = _dot(jnp.concatenate(lhs, axis=0), jnp.concatenate([x_pair, st.astype(BF16)], axis=0))
        x_f = x_pair.astype(F32)
        y_blocks.append(jnp.where(low, yy[:L], yy[L:]) + dsk_ref[:, p * pair:(p + 1) * pair] * x_f)

        w_pair = jnp.where(low, w_end[:, 2 * p:2 * p + 1], w_end[:, 2 * p + 1:2 * p + 2])
        cd = jnp.where(low_row, chunk_decay[:, 2 * p:2 * p + 1], chunk_decay[:, 2 * p + 1:2 * p + 2])
        st_ref[p] = st * cd + _dot_tn(b_g, (x_f * w_pair).astype(BF16))

    y = jnp.concatenate(y_blocks, axis=1)
    y = y * zs_ref[rows, :].astype(F32)
    gw = SSD_INNER // SSD_GROUPS
    nw = nw_ref[...]
    for g in range(SSD_GROUPS):
        cols = slice(g * gw, (g + 1) * gw)
        y_ref[rows, cols] = _rmsnorm_rows(y[:, cols], nw[:, cols]).astype(y_ref.dtype)


def _ssd_mixer(xc, zs, small, dt_bias, a_log, d_skip, norm_w):
    t = xc.shape[0]
    pad = SMALL_COLS - SSD_HEADS
    dtb = jnp.pad(dt_bias.astype(F32), (0, pad)).reshape(1, SMALL_COLS)
    aneg = jnp.pad(-jnp.exp(a_log.astype(F32)), (0, pad)).reshape(1, SMALL_COLS)
    dsk = jnp.repeat(d_skip.astype(F32), SSD_HEAD_DIM).reshape(1, SSD_INNER)
    row = lambda i: (i, 0)
    return _Call(
        "vec", "ssd_mixer", _ssd_kernel,
        (xc, zs, small, dtb, aneg, dsk, norm_w.astype(F32).reshape(1, -1)),
        (pl.BlockSpec((ROW_TILE, SSD_CONV_DIM), row), pl.BlockSpec((ROW_TILE, SSD_INNER), row),
         pl.BlockSpec((ROW_TILE, SMALL_COLS), row), _const_spec((1, SMALL_COLS)),
         _const_spec((1, SMALL_COLS)), _const_spec((1, SSD_INNER)), _const_spec((1, SSD_INNER))),
        (pl.BlockSpec((ROW_TILE, SSD_INNER), row),),
        (jax.ShapeDtypeStruct((t, SSD_INNER), BF16),),
        (pltpu.VMEM((SSD_HEADS // 2, SSD_STATE, 2 * SSD_HEAD_DIM), F32),), zero_scratch=True)


def _gla_level_ref_rows(b_ref, m, C):
    blk = 2 * m
    width = b_ref.shape[1]
    if blk >= SUBLANES:
        parts = []
        for s in range(0, C, blk):
            row = b_ref[s + m - 1:s + m, :]
            parts.append(jnp.broadcast_to(row, (blk, width)))
        return parts[0] if len(parts) == 1 else jnp.concatenate(parts, axis=0)
    b = b_ref[...]
    off = lax.broadcasted_iota(jnp.int32, (C, width), 0) % blk
    out = b
    for o in range(blk):
        delta = o - (m - 1)
        if delta == 0:
            continue
        out = jnp.where(off == o, pltpu.roll(b, delta % C, 0), out)
    return out


def _gla_kernel(qk_ref, v_ref, rs_ref, sm_ref, gw_ref, gb_ref, nw_ref, y_ref, st_ref, b_ref):
    C = GLA_CHUNK

    for sub in range(qk_ref.shape[0] // C):
        _gla_chunk(qk_ref, v_ref, rs_ref, sm_ref, gw_ref, gb_ref, nw_ref, y_ref, st_ref, b_ref.at[sub],
                   slice(sub * C, (sub + 1) * C))


def _gla_chunk(qk_ref, v_ref, rs_ref, sm_ref, gw_ref, gb_ref, nw_ref, y_ref, st_ref, b_ref, rows):
    C = GLA_CHUNK
    pair = 2 * GLA_DK
    qk = qk_ref[rows, :].astype(F32)
    q = qk[:, :GLA_KEY]
    k = qk[:, GLA_KEY:]

    s_f = sm_ref[rows, :]
    s_hi = s_f.astype(BF16)
    s_lo = (s_f - s_hi.astype(F32)).astype(BF16)
    g_f = gw_ref[...]
    g_hi = g_f.astype(BF16)
    g_lo = (g_f - g_hi.astype(F32)).astype(BF16)
    pre = _dot(s_hi, g_hi) + _dot(s_hi, g_lo) + _dot(s_lo, g_hi) + gb_ref[...]
    log_a = -_softplus(-pre) * (1.0 / GLA_TAU)
    b = _cumsum_rows(log_a)
    b_ref[...] = b
    b_last = b[C - 1:C, :]

    rowi = lax.broadcasted_iota(jnp.int32, (C, GLA_KEY), 0)
    lane = lax.broadcasted_iota(jnp.int32, (C, pair), 1)
    head_lanes = (lane < GLA_DK, lane >= GLA_DK)
    ti = lax.broadcasted_iota(jnp.int32, (C, C), 0)
    si = lax.broadcasted_iota(jnp.int32, (C, C), 1)
    ts_xor = ti ^ si

    head_mask = [jnp.where(sel, 1.0, 0.0).astype(BF16) for sel in head_lanes]

    def head_ops(qa, ka, h, mask_k=False):
        cols = slice((h // 2) * pair, (h // 2 + 1) * pair)
        qh = qa[:, cols] * head_mask[h % 2]
        kh = ka[:, cols] * head_mask[h % 2] if mask_k else ka[:, cols]
        return qh, kh

    def pair_scores(qa, ka, j):
        cols = slice(j * pair, (j + 1) * pair)
        qs = jnp.concatenate([qa[:, cols] * head_mask[0], qa[:, cols] * head_mask[1]], axis=0)
        return _dot_nt(qs, ka[:, cols])

    scores = []
    q_bf, k_bf = q.astype(BF16), k.astype(BF16)
    for j in range(GLA_HEADS // 2):
        p = pair_scores(q_bf, k_bf, j)
        scores += [jnp.where(ti == si, p[:C], 0.0), jnp.where(ti == si, p[C:], 0.0)]
    m = C // 2
    while m >= 1:
        blk = 2 * m
        upper = (rowi & m) != 0
        beta = _gla_level_ref_rows(b_ref, m, C)
        e = jnp.exp(jnp.where(upper, b - beta, beta - b))
        qt = jnp.where(upper, q * e, 0.0).astype(BF16)
        kt = jnp.where(upper, 0.0, k * e).astype(BF16)
        for j in range(GLA_HEADS // 2):
            p = pair_scores(qt, kt, j)
            for i, ph in enumerate((p[:C], p[C:])):
                if blk < C:
                    ph = jnp.where(ts_xor < blk, ph, 0.0)
                scores[2 * j + i] = scores[2 * j + i] + ph
        m //= 2

    qe = (q * jnp.exp(b)).astype(BF16)
    k_end = (k * jnp.exp(b_last - b)).astype(BF16)
    st_decay = jnp.exp(b_last)
    nw = nw_ref[...]
    for h in range(GLA_HEADS):
        j = h // 2
        vh = v_ref[rows, h * GLA_DV:(h + 1) * GLA_DV]
        st = st_ref[h]
        qh, kh = head_ops(qe, k_end, h, mask_k=True)
        o = _dot(scores[h].astype(BF16), vh) + _dot_nt(qh, st.astype(BF16))
        st_ref[h] = st * st_decay[:, j * pair:(j + 1) * pair] + _dot_tn(vh, kh)
        o = _rmsnorm_rows(o, nw) * rs_ref[rows, h * GLA_DV:(h + 1) * GLA_DV].astype(F32)
        y_ref[rows, h * GLA_DV:(h + 1) * GLA_DV] = o.astype(y_ref.dtype)


def _gla_mixer(gla_in, small, gate_w, gate_b, norm_w):
    t = gla_in.shape[0]
    gw = jnp.zeros((SMALL_COLS, GLA_KEY), F32).at[SSD_HEADS:SSD_HEADS + GLA_RANK].set(gate_w.astype(F32))
    row = lambda col: (lambda i: (i, col))
    return _Call(
        "vec", "gla_mixer", _gla_kernel,
        (gla_in, gla_in, gla_in, small, gw, gate_b.astype(F32).reshape(1, -1), norm_w.astype(F32).reshape(1, -1)),
        (pl.BlockSpec((ROW_TILE, 2 * GLA_KEY), row(0)), pl.BlockSpec((ROW_TILE, GLA_VAL), row(1)),
         pl.BlockSpec((ROW_TILE, GLA_VAL), row(2)), pl.BlockSpec((ROW_TILE, SMALL_COLS), row(0)),
         _const_spec((SMALL_COLS, GLA_KEY)), _const_spec((1, GLA_KEY)), _const_spec((1, GLA_DV))),
        (pl.BlockSpec((ROW_TILE, GLA_VAL), row(0)),),
        (jax.ShapeDtypeStruct((t, GLA_VAL), BF16),),
        (pltpu.VMEM((GLA_HEADS, GLA_DV, 2 * GLA_DK), F32),
         pltpu.VMEM((ROW_TILE // GLA_CHUNK, GLA_CHUNK, GLA_KEY), F32)),
        zero_scratch=True)


def _att_kernel(q_ref, kh_ref, k_ref, vh_ref, v_ref, o_ref, lse_ref, ktbuf_ref, vbuf_ref, *, nb):
    R = q_ref.shape[0]
    blk = ATT_BLOCK
    for h in range(ATT_HPG):
        cols = slice(h * ATT_DH, (h + 1) * ATT_DH)
        ktbuf_ref[cols, :blk] = kh_ref[:, cols].T
        ktbuf_ref[cols, blk:] = k_ref[:, cols].T
    vbuf_ref[:blk, :] = vh_ref[...]
    vbuf_ref[blk:, :] = v_ref[...]
    qi = lax.broadcasted_iota(jnp.int32, (blk, 2 * blk), 0)
    kj = lax.broadcasted_iota(jnp.int32, (blk, 2 * blk), 1)
    band = jnp.abs(2 * (kj - qi) - blk) <= blk
    first_key = jnp.where(pl.program_id(0) % nb > 0, 0, blk)
    lane = lax.broadcasted_iota(jnp.int32, (blk, LANES), 1)
    for sb in range(R // blk):
        rows = slice(sb * blk, (sb + 1) * blk)
        win = slice(sb * blk, (sb + 2) * blk)
        lse_tile = jnp.zeros((blk, LANES), F32)
        for h in range(ATT_HPG):
            cols = slice(h * ATT_DH, (h + 1) * ATT_DH)
            s = _dot(q_ref[rows, cols], ktbuf_ref[cols, win])
            s = jnp.where(band, s, NEG_INF)
            if sb == 0:
                s = jnp.where(kj >= first_key, s, NEG_INF)
            mx = jnp.max(s, axis=-1, keepdims=True)
            p = jnp.exp(s - mx)
            den = jnp.sum(p, axis=-1, keepdims=True)
            o = _dot(p.astype(BF16), vbuf_ref[win, cols])
            o_ref[rows, cols] = (o * (1.0 / den)).astype(o_ref.dtype)
            lse_tile = jnp.where(lane == h, mx + jnp.log(den), lse_tile)
        lse_ref[rows, :] = lse_tile


def _dilated_attention(q, k, v, g):
    d = ATT_DILATIONS[g]
    sub = q.shape[0]
    assert ATT_WINDOWS[g] // d == ATT_BLOCK and sub % ROW_TILE == 0
    nb = sub // ROW_TILE
    per = ROW_TILE // ATT_BLOCK
    main = pl.BlockSpec((ROW_TILE, ATT_OUT), lambda i: (i % nb, i // nb))
    halo = pl.BlockSpec((ATT_BLOCK, ATT_OUT), lambda i: (jnp.maximum((i % nb) * per - 1, 0), i // nb))
    return _Call(
        "vec", f"dilated_attention_d{d}", functools.partial(_att_kernel, nb=nb), (q, k, k, v, v),
        (main, halo, main, halo, main),
        (main, pl.BlockSpec((ROW_TILE, LANES), lambda i: (i % nb, i // nb))),
        (jax.ShapeDtypeStruct((sub, d * ATT_OUT), BF16), jax.ShapeDtypeStruct((sub, d * LANES), F32)),
        (pltpu.VMEM((ATT_OUT, ROW_TILE + ATT_BLOCK), BF16), pltpu.VMEM((ROW_TILE + ATT_BLOCK, ATT_OUT), BF16)))


def _interleave_rows(src_ref, scr_ref, d, rows, n_blk, base):
    if d == 1:
        return [src_ref[:, a * LANES:(a + 1) * LANES].astype(F32) for a in range(n_blk)]
    for r in range(d):
        for a in range(n_blk):
            c = (r * n_blk + a) * LANES
            scr_ref[pl.ds(base + a * rows + r, rows // d, stride=d), :] = src_ref[:, c:c + LANES].astype(F32)
    return [scr_ref[base + a * rows:base + (a + 1) * rows, :] for a in range(n_blk)]


def _merge_kernel(x_ref, ys_ref, yg_ref, o0_ref, o1_ref, o2_ref, l0_ref, l1_ref, l2_ref, gt_ref,
                  wb_ref, wo_ref, nw_ref, xo_ref, h_ref, oscr_ref, lscr_ref):
    rows = x_ref.shape[0]
    d = x_ref.shape[1]
    slots = [sum(dd > 1 for dd in ATT_DILATIONS[:g]) for g in range(ATT_GROUPS)]
    lses = [_interleave_rows(l_ref, lscr_ref, ATT_DILATIONS[g], rows, 1, slots[g] * rows)[0]
            for g, l_ref in enumerate((l0_ref, l1_ref, l2_ref))]
    mx = jnp.maximum(jnp.maximum(lses[0], lses[1]), lses[2])
    es = [jnp.exp(l - mx) for l in lses]
    inv = 1.0 / (es[0] + es[1] + es[2])
    acc = [jnp.zeros((rows, ATT_DH), F32) for _ in range(ATT_HPG)]
    for g, o_ref in enumerate((o0_ref, o1_ref, o2_ref)):
        o_g = _interleave_rows(o_ref, oscr_ref, ATT_DILATIONS[g], rows, ATT_HPG, slots[g] * ATT_HPG * rows)
        w_g = es[g] * inv
        for h in range(ATT_HPG):
            acc[h] = acc[h] + w_g[:, h:h + 1] * o_g[h]
    y_att = jnp.concatenate([a.astype(BF16) for a in acc], axis=1)

    merged = _sig_gate(gt_ref, 0, d) * _dot(ys_ref[...], wb_ref[:SSD_INNER, :])
    merged = merged + _sig_gate(gt_ref, 1, d) * _dot(yg_ref[...], wb_ref[SSD_INNER:SSD_INNER + GLA_VAL, :])
    merged = merged + _sig_gate(gt_ref, 2, d) * _dot(y_att, wb_ref[SSD_INNER + GLA_VAL:, :])
    x_new = x_ref[...] + _dot(merged.astype(BF16), wo_ref[...])
    xo_ref[...] = x_new
    h_ref[...] = _rmsnorm_rows(x_new, nw_ref[...]).astype(h_ref.dtype)


def _sig_gate(gt_ref, i, d):
    return gt_ref[:, i * d:(i + 1) * d].astype(F32)


def _merge(x2d, y_ssd, y_gla, att_o, att_lse, gates, w_branch, w_out, norm_w, tile0=0):
    t, d = y_ssd.shape[0], x2d.shape[1]
    row = lambda i: (i, 0)
    tile = lambda n: pl.BlockSpec((ROW_TILE, n), row)
    dil_tile = lambda n: [pl.BlockSpec((ROW_TILE // dil, dil * n), row) for dil in ATT_DILATIONS]
    n_dil = sum(dil > 1 for dil in ATT_DILATIONS)
    return _Call(
        "mxu", "merge_out", _merge_kernel,
        (x2d, y_ssd, y_gla, *att_o, *att_lse, gates, w_branch, w_out, norm_w.astype(F32).reshape(1, d)),
        (pl.BlockSpec((ROW_TILE, d), lambda i: (i + tile0, 0)), tile(SSD_INNER), tile(GLA_VAL),
         *dil_tile(ATT_OUT), *dil_tile(LANES), tile(3 * d),
         _const_spec(w_branch.shape), _const_spec(w_out.shape), _const_spec((1, d))),
        (tile(d), tile(d)),
        (jax.ShapeDtypeStruct((t, d), F32), jax.ShapeDtypeStruct((t, d), BF16)),
        (pltpu.VMEM((n_dil * ATT_HPG * ROW_TILE, ATT_DH), F32), pltpu.VMEM((n_dil * ROW_TILE, LANES), F32)))


def _ffn_kernel(x_ref, h_ref, wu_ref, cw_ref, cb_ref, wd_ref, nw_ref, xo_ref, ho_ref, halo_ref):
    rows = x_ref.shape[0]
    hid = wd_ref.shape[0]

    h = h_ref[...]
    gate = _dot(h, wu_ref[:, :hid])
    val = _dot(h, wu_ref[:, hid:])
    halo = halo_ref[...]
    halo_ref[...] = gate[rows - SUBLANES:, :]
    cw = cw_ref[...]
    u = gate * cw[FFN_CONV - 1:FFN_CONV, :] + cb_ref[...]
    for j in range(1, FFN_CONV):
        u = u + _shift_rows(gate, j, halo) * cw[FFN_CONV - 1 - j:FFN_CONV - j, :]
    act = (_silu_of_half(u) * val).astype(BF16)
    x_new = x_ref[...] + _dot(act, wd_ref[...])
    xo_ref[...] = x_new
    ho_ref[...] = _rmsnorm_rows(x_new, nw_ref[...]).astype(ho_ref.dtype)


def _conv_ffn(x2d, h, w_up, conv_w, conv_b, w_down, norm_w, out_dtype):
    t, d = x2d.shape
    hid = w_down.shape[0]
    tile = pl.BlockSpec((ROW_TILE, d), lambda i: (i, 0))
    return _Call(
        "mxu", "conv_ffn", _ffn_kernel,
        (x2d, h, w_up, 0.5 * conv_w.astype(F32), 0.5 * conv_b.astype(F32).reshape(1, hid), w_down,
         norm_w.astype(F32).reshape(1, d)),
        (tile, tile, _const_spec((d, 2 * hid)), _const_spec((FFN_CONV, hid)), _const_spec((1, hid)),
         _const_spec((hid, d)), _const_spec((1, d))),
        (tile, tile),
        (jax.ShapeDtypeStruct((t, d), F32), jax.ShapeDtypeStruct((t, d), out_dtype)),
        (pltpu.VMEM((SUBLANES, hid), F32),), zero_scratch=True)


def _layout_in_proj(w_in_l):
    widths = (SSD_INNER, SSD_CONV_DIM, SSD_HEADS, GLA_KEY, GLA_KEY, GLA_VAL, GLA_RANK, GLA_VAL,
              3 * ATT_HEADS * ATT_DH, None)
    pieces = []
    start = 0
    for wd in widths:
        stop = w_in_l.shape[1] if wd is None else start + wd
        pieces.append(w_in_l[:, start:stop])
        start = stop
    z, xbc, dt, gq, gk, gv, glr, gr, qkv, gates = pieces
    d = w_in_l.shape[0]
    used = SSD_CONV_DIM + SSD_INNER + SSD_HEADS + GLA_RANK
    pad = jnp.zeros((d, W_BLOCK - used), w_in_l.dtype)
    assert gates.shape[1] + 2 * GLA_KEY + 2 * GLA_VAL == W_BLOCK and qkv.shape[1] == W_BLOCK
    return jnp.concatenate([gq * (GLA_DK ** -0.5), gk, gv, 0.5 * gr, 0.5 * gates, qkv,
                            xbc, 0.5 * z, dt, glr, pad], axis=1).astype(BF16)


def _stream(x2d, tile0, seq, rope, layers, norm1_w, final_norm_w):
    (h,) = yield _first_norm(x2d, norm1_w[0], tile0, seq)
    x_s = x2d
    depth = len(layers)
    for l, p in enumerate(layers):
        xc, zs, small = yield _ssd_project(h, p["w_ssd"], p["ssd_conv_w"], p["ssd_conv_b"])
        gla_in, gates = yield _gla_gates_project(h, p["w_gla_gates"])
        att_in = yield _att_project(h, p["w_att"], rope)
        (y_ssd,) = yield _ssd_mixer(xc, zs, small, p["ssd_dt_bias"], p["ssd_a_log"], p["ssd_d"], p["ssd_norm_w"])
        (y_gla,) = yield _gla_mixer(gla_in, small, p["gla_gate_w"], p["gla_gate_b"], p["gla_norm_w"])
        att = []
        for g in range(ATT_GROUPS):
            att.append((yield _dilated_attention(att_in[g], att_in[ATT_GROUPS + g], att_in[2 * ATT_GROUPS + g], g)))
        x_s, h = yield _merge(x_s, y_ssd, y_gla, [a[0] for a in att], [a[1] for a in att], gates,
                              p["w_branch"], p["w_out"], p["norm2_w"], tile0 if l == 0 else 0)
        last = l == depth - 1
        x_s, h = yield _conv_ffn(x_s, h, p["ffn_up"], p["ffn_conv_w"], p["ffn_conv_b"], p["ffn_down"],
                                 final_norm_w if last else norm1_w[l + 1], F32 if last else BF16)
    return h


def kernel(x, norm1_w, w_in, ssd_conv_w, ssd_conv_b, ssd_dt_bias, ssd_a_log, ssd_d, ssd_norm_w,
           gla_gate_w, gla_gate_b, gla_norm_w, w_branch, w_out, norm2_w, ffn_up, ffn_conv_w,
           ffn_conv_b, ffn_down, final_norm_w):
    bsz, seq, d = x.shape
    depth = w_in.shape[0]
    assert seq % (ROW_TILE * max(ATT_DILATIONS)) == 0
    assert ROW_TILE % SSD_CHUNK == 0 and ROW_TILE % GLA_CHUNK == 0
    rope = _rope_table(seq)
    layers = []
    for l in range(depth):
        w = _layout_in_proj(w_in[l])
        layers.append({
            "w_ssd": w, "w_gla_gates": w, "w_att": w,
            "ssd_conv_w": ssd_conv_w[l], "ssd_conv_b": ssd_conv_b[l], "ssd_dt_bias": ssd_dt_bias[l],
            "ssd_a_log": ssd_a_log[l], "ssd_d": ssd_d[l], "ssd_norm_w": ssd_norm_w[l],
            "gla_gate_w": gla_gate_w[l], "gla_gate_b": gla_gate_b[l], "gla_norm_w": gla_norm_w[l],
            "w_branch": w_branch[l].astype(BF16), "w_out": w_out[l].astype(BF16), "norm2_w": norm2_w[l],
            "ffn_up": ffn_up[l].astype(BF16), "ffn_conv_w": ffn_conv_w[l], "ffn_conv_b": ffn_conv_b[l],
            "ffn_down": ffn_down[l].astype(BF16),
        })
    steps = seq // ROW_TILE
    x2d = x.reshape(bsz * seq, d)
    streams = [_stream(x2d, b * steps, seq, rope, layers, norm1_w, final_norm_w) for b in range(bsz)]
    return jnp.stack(_interleave(streams, steps))
```

```python
import functools
from typing import Callable, NamedTuple

import jax
import jax.numpy as jnp
import numpy as np
from jax import lax
from jax.experimental import pallas as pl
from jax.experimental.pallas import tpu as pltpu

F32 = jnp.float32
BF16 = jnp.bfloat16

LANES = 128
SUBLANES = 8
VMEM_LIMIT_BYTES = 56 * 1024 * 1024

RMS_EPS = 1e-6
SSD_HEADS = 16
SSD_HEAD_DIM = 64
SSD_INNER = SSD_HEADS * SSD_HEAD_DIM
SSD_STATE = 128
SSD_GROUPS = 4
SSD_CONV = 4
SSD_CONV_DIM = SSD_INNER + 2 * SSD_GROUPS * SSD_STATE
GLA_HEADS = 4
GLA_DK = 64
GLA_DV = 128
GLA_KEY = GLA_HEADS * GLA_DK
GLA_VAL = GLA_HEADS * GLA_DV
GLA_RANK = 16
GLA_TAU = 16.0
ATT_DILATIONS = (1, 4, 16)
ATT_WINDOWS = (128, 512, 2048)
ATT_GROUPS = 3
ATT_HPG = 4
ATT_HEADS = ATT_GROUPS * ATT_HPG
ATT_DH = 128
ATT_OUT = ATT_HPG * ATT_DH
ATT_BLOCK = 128
ROPE_THETA = 10000.0
FFN_CONV = 3

ROW_TILE = 512
PROJ_COLS = 512
SSD_CHUNK = 128
GLA_CHUNK = 128
SMALL_COLS = LANES
W_BLOCK = 3 * ATT_HEADS * ATT_DH
W_GLA_GATES_BLOCK, W_ATT_BLOCK, W_SSD_BLOCK = 0, 1, 2

NEG_INF = float("-inf")


def _params(sem):
    return pltpu.CompilerParams(dimension_semantics=sem, vmem_limit_bytes=VMEM_LIMIT_BYTES)


def _const_spec(shape, col_block=0):
    index = (0,) * (len(shape) - 1) + (col_block,)
    return pl.BlockSpec(shape, lambda *_: index, pipeline_mode=pl.Buffered(1))


class _Call(NamedTuple):
    kind: str
    name: str
    body: Callable
    args: tuple
    in_specs: tuple
    out_specs: tuple
    out_shape: tuple
    scratch_shapes: tuple = ()
    zero_scratch: bool = False


def _run(calls, steps):
    n_in = [len(c.args) for c in calls]
    n_out = [len(c.out_shape) for c in calls]
    n_scr = [len(c.scratch_shapes) for c in calls]

    def body(*refs):
        ins = refs[:sum(n_in)]
        outs = refs[sum(n_in):sum(n_in) + sum(n_out)]
        scrs = refs[sum(n_in) + sum(n_out):]

        @pl.when(pl.program_id(0) == 0)
        def _():
            s = 0
            for c, ns in zip(calls, n_scr):
                if c.zero_scratch:
                    for ref in scrs[s:s + ns]:
                        ref[...] = jnp.zeros_like(ref)
                s += ns

        i = o = s = 0
        for c, ni, no, ns in zip(calls, n_in, n_out, n_scr):
            c.body(*ins[i:i + ni], *outs[o:o + no], *scrs[s:s + ns])
            i, o, s = i + ni, o + no, s + ns

    flat = pl.pallas_call(
        body,
        grid=(steps,),
        in_specs=[sp for c in calls for sp in c.in_specs],
        out_specs=[sp for c in calls for sp in c.out_specs],
        out_shape=[sh for c in calls for sh in c.out_shape],
        scratch_shapes=[sc for c in calls for sc in c.scratch_shapes],
        compiler_params=_params(("arbitrary",)),
        name="__".join(c.name for c in calls),
    )(*[a for c in calls for a in c.args])
    outs, o = [], 0
    for no in n_out:
        outs.append(list(flat[o:o + no]))
        o += no
    return outs


def _interleave(streams, steps):
    pending = [next(st) for st in streams]
    done = [None] * len(streams)

    def advance(i, outs):
        try:
            pending[i] = streams[i].send(outs)
        except StopIteration as stop:
            pending[i], done[i] = None, stop.value

    while any(p is not None for p in pending):
        live = [i for i, p in enumerate(pending) if p is not None]
        group = live[:1]
        for i in live[1:]:
            if pending[i].kind != pending[group[0]].kind:
                group.append(i)
                break
        outs = _run([pending[i] for i in group], steps)
        for i, o in zip(group, outs):
            advance(i, o)
    return done


def _sigmoid_of_half(u):
    return 0.5 * jnp.tanh(u) + 0.5


def _silu_of_half(u):
    return u * jnp.tanh(u) + u


def _softplus(x):
    return jnp.maximum(x, 0.0) + jnp.log(1.0 + jnp.exp(-jnp.abs(x)))


def _dot(a, b):
    return jnp.dot(a, b, preferred_element_type=F32)


def _dot_nt(a, b):
    return lax.dot_general(a, b, (((1,), (1,)), ((), ())), preferred_element_type=F32)


def _dot_tn(a, b):
    return lax.dot_general(a, b, (((0,), (0,)), ((), ())), preferred_element_type=F32)


def _shift_rows(x, j, halo):
    n, c = x.shape
    x3 = x.reshape(n // SUBLANES, SUBLANES, c)
    prev = jnp.concatenate([halo[None], x3[:-1]], axis=0)
    sub = lax.broadcasted_iota(jnp.int32, x3.shape, 1)
    return pltpu.roll(jnp.where(sub >= SUBLANES - j, prev, x3), j, 1).reshape(n, c)


def _cumsum_rows(x):
    n = x.shape[0]
    row = lax.broadcasted_iota(jnp.int32, x.shape, 0)
    k = 1
    while k < n:
        x = x + jnp.where(row >= k, pltpu.roll(x, k, 0), 0.0)
        k *= 2
    return x


def _rmsnorm_rows(x, w):
    return x * lax.rsqrt(jnp.mean(x * x, axis=-1, keepdims=True) + RMS_EPS) * w


def _rope_table_kernel(inv_ref, cos_ref, sin_ref):
    rows = cos_ref.shape[0]
    base = pl.program_id(0) * rows
    pos = (base + lax.broadcasted_iota(jnp.int32, (rows, ATT_DH), 0)).astype(F32)
    lane = lax.broadcasted_iota(jnp.int32, (rows, ATT_DH), 1)
    ang = pos * inv_ref[...]
    cos_ref[...] = jnp.cos(ang)
    s = jnp.sin(ang)
    sin_ref[...] = jnp.where(lane < ATT_DH // 2, -s, s)


def _rope_table(seq):
    half = ATT_DH // 2
    inv = ROPE_THETA ** (-jnp.arange(half, dtype=F32) / half)
    inv = jnp.concatenate([inv, inv]).reshape(1, ATT_DH)
    rows = min(seq, 1024)
    return pl.pallas_call(
        _rope_table_kernel,
        grid=(seq // rows,),
        in_specs=[pl.BlockSpec((1, ATT_DH), lambda i: (0, 0))],
        out_specs=[pl.BlockSpec((rows, ATT_DH), lambda i: (i, 0))] * 2,
        out_shape=[jax.ShapeDtypeStruct((seq, ATT_DH), F32)] * 2,
        compiler_params=_params(("parallel",)),
        name="rope_table",
    )(inv)


def _norm_kernel(x_ref, w_ref, h_ref):
    h_ref[...] = _rmsnorm_rows(x_ref[...], w_ref[...]).astype(h_ref.dtype)


def _first_norm(x2d, w, tile0, seq):
    d = x2d.shape[1]
    return _Call("mxu", "first_norm", _norm_kernel, (x2d, w.astype(F32).reshape(1, d)),
                 (pl.BlockSpec((ROW_TILE, d), lambda i: (i + tile0, 0)), _const_spec((1, d))),
                 (pl.BlockSpec((ROW_TILE, d), lambda i: (i, 0)),), (jax.ShapeDtypeStruct((seq, d), BF16),))


def _ssd_proj_kernel(h_ref, w_ref, cw_ref, cb_ref, xc_ref, zs_ref, sm_ref, halo_ref):
    rows = h_ref.shape[0]

    h = h_ref[...]
    for j in range(SSD_CONV_DIM // PROJ_COLS):
        cols = slice(j * PROJ_COLS, (j + 1) * PROJ_COLS)
        acc = _dot(h, w_ref[:, cols])
        halo = halo_ref[:, cols]
        halo_ref[:, cols] = acc[rows - SUBLANES:, :]
        cw = cw_ref[:, cols]
        u = acc * cw[SSD_CONV - 1:SSD_CONV, :] + cb_ref[:, cols]
        for s in range(1, SSD_CONV):
            u = u + _shift_rows(acc, s, halo) * cw[SSD_CONV - 1 - s:SSD_CONV - s, :]
        xc_ref[:, cols] = _silu_of_half(u).astype(xc_ref.dtype)
    for j in range(SSD_INNER // PROJ_COLS):
        cols = slice(j * PROJ_COLS, (j + 1) * PROJ_COLS)
        wcols = slice(SSD_CONV_DIM + j * PROJ_COLS, SSD_CONV_DIM + (j + 1) * PROJ_COLS)
        zs_ref[:, cols] = _silu_of_half(_dot(h, w_ref[:, wcols])).astype(zs_ref.dtype)
    sm_ref[...] = _dot(h, w_ref[:, SSD_CONV_DIM + SSD_INNER:SSD_CONV_DIM + SSD_INNER + SMALL_COLS])


def _ssd_project(h, w, conv_w, conv_b):
    t, d = h.shape
    row = lambda i: (i, 0)
    return _Call(
        "mxu", "proj_ssd", _ssd_proj_kernel,
        (h, w, 0.5 * conv_w.astype(F32), 0.5 * conv_b.astype(F32).reshape(1, -1)),
        (pl.BlockSpec((ROW_TILE, d), row), _const_spec((d, W_BLOCK), W_SSD_BLOCK),
         _const_spec((SSD_CONV, SSD_CONV_DIM)), _const_spec((1, SSD_CONV_DIM))),
        (pl.BlockSpec((ROW_TILE, SSD_CONV_DIM), row), pl.BlockSpec((ROW_TILE, SSD_INNER), row),
         pl.BlockSpec((ROW_TILE, SMALL_COLS), row)),
        (jax.ShapeDtypeStruct((t, SSD_CONV_DIM), BF16), jax.ShapeDtypeStruct((t, SSD_INNER), BF16),
         jax.ShapeDtypeStruct((t, SMALL_COLS), F32)),
        (pltpu.VMEM((SUBLANES, SSD_CONV_DIM), F32),), zero_scratch=True)


def _gla_gates_proj_kernel(h_ref, w_ref, gla_ref, gt_ref):
    h = h_ref[...]
    n_gla = gla_ref.shape[1]
    for j in range(w_ref.shape[1] // PROJ_COLS):
        c0 = j * PROJ_COLS
        acc = _dot(h, w_ref[:, c0:c0 + PROJ_COLS])
        if c0 >= n_gla:
            gt_ref[:, c0 - n_gla:c0 - n_gla + PROJ_COLS] = _sigmoid_of_half(acc).astype(gt_ref.dtype)
        else:
            if c0 >= 2 * GLA_KEY + GLA_VAL:
                acc = _silu_of_half(acc)
            gla_ref[:, c0:c0 + PROJ_COLS] = acc.astype(gla_ref.dtype)


def _gla_gates_project(h, w):
    t, d = h.shape
    n_gla = 2 * GLA_KEY + 2 * GLA_VAL
    n_gates = W_BLOCK - n_gla
    row = lambda i: (i, 0)
    return _Call(
        "mxu", "proj_gla_gates", _gla_gates_proj_kernel, (h, w),
        (pl.BlockSpec((ROW_TILE, d), row), _const_spec((d, W_BLOCK), W_GLA_GATES_BLOCK)),
        (pl.BlockSpec((ROW_TILE, n_gla), row), pl.BlockSpec((ROW_TILE, n_gates), row)),
        (jax.ShapeDtypeStruct((t, n_gla), BF16), jax.ShapeDtypeStruct((t, n_gates), BF16)))


def _att_proj_kernel(h_ref, w_ref, cos_ref, sin_ref, *rest):
    out_refs, scr_ref = rest[:-1], rest[-1]
    h = h_ref[...]
    rows = h.shape[0]
    cos = cos_ref[...]
    sin = sin_ref[...]
    slot = 0
    for kind in range(3):
        for g, d in enumerate(ATT_DILATIONS):
            c0 = (kind * ATT_GROUPS + g) * ATT_OUT
            acc = _dot(h, w_ref[:, c0:c0 + ATT_OUT])
            o_ref = out_refs[kind * ATT_GROUPS + g]
            for a in range(ATT_HPG):
                xh = acc[:, a * ATT_DH:(a + 1) * ATT_DH]
                if kind < 2:
                    xh = xh * cos + pltpu.roll(xh, ATT_DH // 2, 1) * sin
                if kind == 0:
                    xh = xh * (ATT_DH ** -0.5)
                if d == 1:
                    o_ref[:, a * ATT_DH:(a + 1) * ATT_DH] = xh.astype(o_ref.dtype)
                    continue
                scr_ref[slot * rows:(slot + 1) * rows, :] = xh
                for r in range(d):
                    c = r * ATT_OUT + a * ATT_DH
                    o_ref[:, c:c + ATT_DH] = (
                        scr_ref[pl.ds(slot * rows + r, rows // d, stride=d), :].astype(o_ref.dtype))
                slot += 1


def _att_project(h, w_qkv, rope):
    t, dm = h.shape
    row = lambda i: (i, 0)
    rope_spec = pl.BlockSpec((ROW_TILE, ATT_DH), row)
    out_specs, out_shape = [], []
    for _ in range(3):
        for d in ATT_DILATIONS:
            out_specs.append(pl.BlockSpec((ROW_TILE // d, d * ATT_OUT), row))
            out_shape.append(jax.ShapeDtypeStruct((t // d, d * ATT_OUT), BF16))
    n_slots = 3 * sum(d > 1 for d in ATT_DILATIONS) * ATT_HPG
    return _Call(
        "mxu", "proj_att", _att_proj_kernel, (h, w_qkv, *rope),
        (pl.BlockSpec((ROW_TILE, dm), row), _const_spec((dm, W_BLOCK), W_ATT_BLOCK), rope_spec, rope_spec),
        tuple(out_specs), tuple(out_shape),
        (pltpu.VMEM((n_slots * ROW_TILE, ATT_DH), F32),))


def _ssd_kernel(xc_ref, zs_ref, sm_ref, dtb_ref, aneg_ref, dsk_ref, nw_ref, y_ref, st_ref):
    L = SSD_CHUNK
    pair = 2 * SSD_HEAD_DIM

    ti = lax.broadcasted_iota(jnp.int32, (L, L), 0)
    si = lax.broadcasted_iota(jnp.int32, (L, L), 1)
    causal = ti >= si
    low = lax.broadcasted_iota(jnp.int32, (L, pair), 1) < SSD_HEAD_DIM
    low_row = lax.broadcasted_iota(jnp.int32, (1, pair), 1) < SSD_HEAD_DIM
    for sub in range(xc_ref.shape[0] // L):
        rows = slice(sub * L, (sub + 1) * L)
        _ssd_chunk(xc_ref, zs_ref, sm_ref, dtb_ref, aneg_ref, dsk_ref, nw_ref, y_ref, st_ref, rows,
                   causal, low, low_row)


def _ssd_chunk(xc_ref, zs_ref, sm_ref, dtb_ref, aneg_ref, dsk_ref, nw_ref, y_ref, st_ref, rows,
               causal, low, low_row):
    L = SSD_CHUNK
    n_state = SSD_STATE
    pair = 2 * SSD_HEAD_DIM
    hpg = SSD_HEADS // SSD_GROUPS
    dt = _softplus(sm_ref[rows, :] + dtb_ref[...])
    a_cum = _cumsum_rows(dt * aneg_ref[...])
    a_last = a_cum[L - 1:L, :]
    w_end = jnp.exp(a_last - a_cum) * dt
    e_cum = jnp.exp(a_cum)
    chunk_decay = jnp.exp(a_last)
    a_cum_t = a_cum.T
    dt_t = dt.T

    b_off = SSD_INNER
    c_off = SSD_INNER + SSD_GROUPS * n_state
    y_blocks = []
    for p in range(SSD_HEADS // 2):
        g = (2 * p) // hpg
        if (2 * p) % hpg == 0:
            b_g = xc_ref[rows, b_off + g * n_state:b_off + (g + 1) * n_state]
            c_bf = xc_ref[rows, c_off + g * n_state:c_off + (g + 1) * n_state]
            cb = _dot_nt(c_bf, b_g)
            c_g = c_bf.astype(F32)
        x_pair = xc_ref[rows, p * pair:(p + 1) * pair]
        st = st_ref[p]
        lhs = []
        for h in (2 * p, 2 * p + 1):
            seg = a_cum[:, h:h + 1] - a_cum_t[h:h + 1, :]
            dec = jnp.exp(jnp.where(causal, seg, NEG_INF))
            m = (cb * dec * dt_t[h:h + 1, :]).astype(BF16)
            c_s = (c_g * e_cum[:, h:h + 1]).astype(BF16)
            lhs.append(jnp.concatenate([m, c_s], axis=1))
        yy = _dot(jnp.concatenate(lhs, axis=0), jnp.concatenate([x_pair, st.astype(BF16)], axis=0))
        x_f = x_pair.astype(F32)
        y_blocks.append(jnp.where(low, yy[:L], yy[L:]) + dsk_ref[:, p * pair:(p + 1) * pair] * x_f)

        w_pair = jnp.where(low, w_end[:, 2 * p:2 * p + 1], w_end[:, 2 * p + 1:2 * p + 2])
        cd = jnp.where(low_row, chunk_decay[:, 2 * p:2 * p + 1], chunk_decay[:, 2 * p + 1:2 * p + 2])
        st_ref[p] = st * cd + _dot_tn(b_g, (x_f * w_pair).astype(BF16))

    y = jnp.concatenate(y_blocks, axis=1)
    y = y * zs_ref[rows, :].astype(F32)
    gw = SSD_INNER // SSD_GROUPS
    nw = nw_ref[...]
    for g in range(SSD_GROUPS):
        cols = slice(g * gw, (g + 1) * gw)
        y_ref[rows, cols] = _rmsnorm_rows(y[:, cols], nw[:, cols]).astype(y_ref.dtype)


def _ssd_mixer(xc, zs, small, dt_bias, a_log, d_skip, norm_w):
    t = xc.shape[0]
    pad = SMALL_COLS - SSD_HEADS
    dtb = jnp.pad(dt_bias.astype(F32), (0, pad)).reshape(1, SMALL_COLS)
    aneg = jnp.pad(-jnp.exp(a_log.astype(F32)), (0, pad)).reshape(1, SMALL_COLS)
    dsk = jnp.repeat(d_skip.astype(F32), SSD_HEAD_DIM).reshape(1, SSD_INNER)
    row = lambda i: (i, 0)
    return _Call(
        "vec", "ssd_mixer", _ssd_kernel,
        (xc, zs, small, dtb, aneg, dsk, norm_w.astype(F32).reshape(1, -1)),
        (pl.BlockSpec((ROW_TILE, SSD_CONV_DIM), row), pl.BlockSpec((ROW_TILE, SSD_INNER), row),
         pl.BlockSpec((ROW_TILE, SMALL_COLS), row), _const_spec((1, SMALL_COLS)),
         _const_spec((1, SMALL_COLS)), _const_spec((1, SSD_INNER)), _const_spec((1, SSD_INNER))),
        (pl.BlockSpec((ROW_TILE, SSD_INNER), row),),
        (jax.ShapeDtypeStruct((t, SSD_INNER), BF16),),
        (pltpu.VMEM((SSD_HEADS // 2, SSD_STATE, 2 * SSD_HEAD_DIM), F32),), zero_scratch=True)


def _gla_level_ref_rows(b_ref, m, C):
    blk = 2 * m
    width = b_ref.shape[1]
    if blk >= SUBLANES:
        parts = []
        for s in range(0, C, blk):
            row = b_ref[s + m - 1:s + m, :]
            parts.append(jnp.broadcast_to(row, (blk, width)))
        return parts[0] if len(parts) == 1 else jnp.concatenate(parts, axis=0)
    b = b_ref[...]
    off = lax.broadcasted_iota(jnp.int32, (C, width), 0) % blk
    out = b
    for o in range(blk):
        delta = o - (m - 1)
        if delta == 0:
            continue
        out = jnp.where(off == o, pltpu.roll(b, delta % C, 0), out)
    return out


def _gla_kernel(qk_ref, v_ref, rs_ref, sm_ref, gw_ref, gb_ref, nw_ref, y_ref, st_ref, b_ref):
    C = GLA_CHUNK

    for sub in range(qk_ref.shape[0] // C):
        _gla_chunk(qk_ref, v_ref, rs_ref, sm_ref, gw_ref, gb_ref, nw_ref, y_ref, st_ref, b_ref.at[sub],
                   slice(sub * C, (sub + 1) * C))


def _gla_chunk(qk_ref, v_ref, rs_ref, sm_ref, gw_ref, gb_ref, nw_ref, y_ref, st_ref, b_ref, rows):
    C = GLA_CHUNK
    pair = 2 * GLA_DK
    qk = qk_ref[rows, :].astype(F32)
    q = qk[:, :GLA_KEY]
    k = qk[:, GLA_KEY:]

    s_f = sm_ref[rows, :]
    s_hi = s_f.astype(BF16)
    s_lo = (s_f - s_hi.astype(F32)).astype(BF16)
    g_f = gw_ref[...]
    g_hi = g_f.astype(BF16)
    g_lo = (g_f - g_hi.astype(F32)).astype(BF16)
    pre = _dot(s_hi, g_hi) + _dot(s_hi, g_lo) + _dot(s_lo, g_hi) + gb_ref[...]
    log_a = -_softplus(-pre) * (1.0 / GLA_TAU)
    b = _cumsum_rows(log_a)
    b_ref[...] = b
    b_last = b[C - 1:C, :]

    rowi = lax.broadcasted_iota(jnp.int32, (C, GLA_KEY), 0)
    lane = lax.broadcasted_iota(jnp.int32, (C, pair), 1)
    head_lanes = (lane < GLA_DK, lane >= GLA_DK)
    ti = lax.broadcasted_iota(jnp.int32, (C, C), 0)
    si = lax.broadcasted_iota(jnp.int32, (C, C), 1)
    ts_xor = ti ^ si

    head_mask = [jnp.where(sel, 1.0, 0.0).astype(BF16) for sel in head_lanes]

    def head_ops(qa, ka, h, mask_k=False):
        cols = slice((h // 2) * pair, (h // 2 + 1) * pair)
        qh = qa[:, cols] * head_mask[h % 2]
        kh = ka[:, cols] * head_mask[h % 2] if mask_k else ka[:, cols]
        return qh, kh

    def pair_scores(qa, ka, j):
        cols = slice(j * pair, (j + 1) * pair)
        qs = jnp.concatenate([qa[:, cols] * head_mask[0], qa[:, cols] * head_mask[1]], axis=0)
        return _dot_nt(qs, ka[:, cols])

    scores = []
    q_bf, k_bf = q.astype(BF16), k.astype(BF16)
    for j in range(GLA_HEADS // 2):
        p = pair_scores(q_bf, k_bf, j)
        scores += [jnp.where(ti == si, p[:C], 0.0), jnp.where(ti == si, p[C:], 0.0)]
    m = C // 2
    while m >= 1:
        blk = 2 * m
        upper = (rowi & m) != 0
        beta = _gla_level_ref_rows(b_ref, m, C)
        e = jnp.exp(jnp.where(upper, b - beta, beta - b))
        qt = jnp.where(upper, q * e, 0.0).astype(BF16)
        kt = jnp.where(upper, 0.0, k * e).astype(BF16)
        for j in range(GLA_HEADS // 2):
            p = pair_scores(qt, kt, j)
            for i, ph in enumerate((p[:C], p[C:])):
                if blk < C:
                    ph = jnp.where(ts_xor < blk, ph, 0.0)
                scores[2 * j + i] = scores[2 * j + i] + ph
        m //= 2

    qe = (q * jnp.exp(b)).astype(BF16)
    k_end = (k * jnp.exp(b_last - b)).astype(BF16)
    st_decay = jnp.exp(b_last)
    nw = nw_ref[...]
    for h in range(GLA_HEADS):
        j = h // 2
        vh = v_ref[rows, h * GLA_DV:(h + 1) * GLA_DV]
        st = st_ref[h]
        qh, kh = head_ops(qe, k_end, h, mask_k=True)
        o = _dot(scores[h].astype(BF16), vh) + _dot_nt(qh, st.astype(BF16))
        st_ref[h] = st * st_decay[:, j * pair:(j + 1) * pair] + _dot_tn(vh, kh)
        o = _rmsnorm_rows(o, nw) * rs_ref[rows, h * GLA_DV:(h + 1) * GLA_DV].astype(F32)
        y_ref[rows, h * GLA_DV:(h + 1) * GLA_DV] = o.astype(y_ref.dtype)


def _gla_mixer(gla_in, small, gate_w, gate_b, norm_w):
    t = gla_in.shape[0]
    gw = jnp.zeros((SMALL_COLS, GLA_KEY), F32).at[SSD_HEADS:SSD_HEADS + GLA_RANK].set(gate_w.astype(F32))
    row = lambda col: (lambda i: (i, col))
    return _Call(
        "vec", "gla_mixer", _gla_kernel,
        (gla_in, gla_in, gla_in, small, gw, gate_b.astype(F32).reshape(1, -1), norm_w.astype(F32).reshape(1, -1)),
        (pl.BlockSpec((ROW_TILE, 2 * GLA_KEY), row(0)), pl.BlockSpec((ROW_TILE, GLA_VAL), row(1)),
         pl.BlockSpec((ROW_TILE, GLA_VAL), row(2)), pl.BlockSpec((ROW_TILE, SMALL_COLS), row(0)),
         _const_spec((SMALL_COLS, GLA_KEY)), _const_spec((1, GLA_KEY)), _const_spec((1, GLA_DV))),
        (pl.BlockSpec((ROW_TILE, GLA_VAL), row(0)),),
        (jax.ShapeDtypeStruct((t, GLA_VAL), BF16),),
        (pltpu.VMEM((GLA_HEADS, GLA_DV, 2 * GLA_DK), F32),
         pltpu.VMEM((ROW_TILE // GLA_CHUNK, GLA_CHUNK, GLA_KEY), F32)),
        zero_scratch=True)


def _att_kernel(q_ref, kh_ref, k_ref, vh_ref, v_ref, o_ref, lse_ref, ktbuf_ref, vbuf_ref, *, nb):
    R = q_ref.shape[0]
    blk = ATT_BLOCK
    for h in range(ATT_HPG):
        cols = slice(h * ATT_DH, (h + 1) * ATT_DH)
        ktbuf_ref[cols, :blk] = kh_ref[:, cols].T
        ktbuf_ref[cols, blk:] = k_ref[:, cols].T
    vbuf_ref[:blk, :] = vh_ref[...]
    vbuf_ref[blk:, :] = v_ref[...]
    qi = lax.broadcasted_iota(jnp.int32, (blk, 2 * blk), 0)
    kj = lax.broadcasted_iota(jnp.int32, (blk, 2 * blk), 1)
    band = jnp.abs(2 * (kj - qi) - blk) <= blk
    first_key = jnp.where(pl.program_id(0) % nb > 0, 0, blk)
    lane = lax.broadcasted_iota(jnp.int32, (blk, LANES), 1)
    for sb in range(R // blk):
        rows = slice(sb * blk, (sb + 1) * blk)
        win = slice(sb * blk, (sb + 2) * blk)
        lse_tile = jnp.zeros((blk, LANES), F32)
        for h in range(ATT_HPG):
            cols = slice(h * ATT_DH, (h + 1) * ATT_DH)
            s = _dot(q_ref[rows, cols], ktbuf_ref[cols, win])
            s = jnp.where(band, s, NEG_INF)
            if sb == 0:
                s = jnp.where(kj >= first_key, s, NEG_INF)
            mx = jnp.max(s, axis=-1, keepdims=True)
            p = jnp.exp(s - mx)
            den = jnp.sum(p, axis=-1, keepdims=True)
            o = _dot(p.astype(BF16), vbuf_ref[win, cols])
            o_ref[rows, cols] = (o * (1.0 / den)).astype(o_ref.dtype)
            lse_tile = jnp.where(lane == h, mx + jnp.log(den), lse_tile)
        lse_ref[rows, :] = lse_tile


def _dilated_attention(q, k, v, g):
    d = ATT_DILATIONS[g]
    sub = q.shape[0]
    assert ATT_WINDOWS[g] // d == ATT_BLOCK and sub % ROW_TILE == 0
    nb = sub // ROW_TILE
    per = ROW_TILE // ATT_BLOCK
    main = pl.BlockSpec((ROW_TILE, ATT_OUT), lambda i: (i % nb, i // nb))
    halo = pl.BlockSpec((ATT_BLOCK, ATT_OUT), lambda i: (jnp.maximum((i % nb) * per - 1, 0), i // nb))
    return _Call(
        "vec", f"dilated_attention_d{d}", functools.partial(_att_kernel, nb=nb), (q, k, k, v, v),
        (main, halo, main, halo, main),
        (main, pl.BlockSpec((ROW_TILE, LANES), lambda i: (i % nb, i // nb))),
        (jax.ShapeDtypeStruct((sub, d * ATT_OUT), BF16), jax.ShapeDtypeStruct((sub, d * LANES), F32)),
        (pltpu.VMEM((ATT_OUT, ROW_TILE + ATT_BLOCK), BF16), pltpu.VMEM((ROW_TILE + ATT_BLOCK, ATT_OUT), BF16)))


def _interleave_rows(src_ref, scr_ref, d, rows, n_blk, base):
    if d == 1:
        return [src_ref[:, a * LANES:(a + 1) * LANES].astype(F32) for a in range(n_blk)]
    for r in range(d):
        for a in range(n_blk):
            c = (r * n_blk + a) * LANES
            scr_ref[pl.ds(base + a * rows + r, rows // d, stride=d), :] = src_ref[:, c:c + LANES].astype(F32)
    return [scr_ref[base + a * rows:base + (a + 1) * rows, :] for a in range(n_blk)]


def _merge_kernel(x_ref, ys_ref, yg_ref, o0_ref, o1_ref, o2_ref, l0_ref, l1_ref, l2_ref, gt_ref,
                  wb_ref, wo_ref, nw_ref, xo_ref, h_ref, oscr_ref, lscr_ref):
    rows = x_ref.shape[0]
    d = x_ref.shape[1]
    slots = [sum(dd > 1 for dd in ATT_DILATIONS[:g]) for g in range(ATT_GROUPS)]
    lses = [_interleave_rows(l_ref, lscr_ref, ATT_DILATIONS[g], rows, 1, slots[g] * rows)[0]
            for g, l_ref in enumerate((l0_ref, l1_ref, l2_ref))]
    mx = jnp.maximum(jnp.maximum(lses[0], lses[1]), lses[2])
    es = [jnp.exp(l - mx) for l in lses]
    inv = 1.0 / (es[0] + es[1] + es[2])
    acc = [jnp.zeros((rows, ATT_DH), F32) for _ in range(ATT_HPG)]
    for g, o_ref in enumerate((o0_ref, o1_ref, o2_ref)):
        o_g = _interleave_rows(o_ref, oscr_ref, ATT_DILATIONS[g], rows, ATT_HPG, slots[g] * ATT_HPG * rows)
        w_g = es[g] * inv
        for h in range(ATT_HPG):
            acc[h] = acc[h] + w_g[:, h:h + 1] * o_g[h]
    y_att = jnp.concatenate([a.astype(BF16) for a in acc], axis=1)

    merged = _sig_gate(gt_ref, 0, d) * _dot(ys_ref[...], wb_ref[:SSD_INNER, :])
    merged = merged + _sig_gate(gt_ref, 1, d) * _dot(yg_ref[...], wb_ref[SSD_INNER:SSD_INNER + GLA_VAL, :])
    merged = merged + _sig_gate(gt_ref, 2, d) * _dot(y_att, wb_ref[SSD_INNER + GLA_VAL:, :])
    x_new = x_ref[...] + _dot(merged.astype(BF16), wo_ref[...])
    xo_ref[...] = x_new
    h_ref[...] = _rmsnorm_rows(x_new, nw_ref[...]).astype(h_ref.dtype)


def _sig_gate(gt_ref, i, d):
    return gt_ref[:, i * d:(i + 1) * d].astype(F32)


def _merge(x2d, y_ssd, y_gla, att_o, att_lse, gates, w_branch, w_out, norm_w, tile0=0):
    t, d = y_ssd.shape[0], x2d.shape[1]
    row = lambda i: (i, 0)
    tile = lambda n: pl.BlockSpec((ROW_TILE, n), row)
    dil_tile = lambda n: [pl.BlockSpec((ROW_TILE // dil, dil * n), row) for dil in ATT_DILATIONS]
    n_dil = sum(dil > 1 for dil in ATT_DILATIONS)
    return _Call(
        "mxu", "merge_out", _merge_kernel,
        (x2d, y_ssd, y_gla, *att_o, *att_lse, gates, w_branch, w_out, norm_w.astype(F32).reshape(1, d)),
        (pl.BlockSpec((ROW_TILE, d), lambda i: (i + tile0, 0)), tile(SSD_INNER), tile(GLA_VAL),
         *dil_tile(ATT_OUT), *dil_tile(LANES), tile(3 * d),
         _const_spec(w_branch.shape), _const_spec(w_out.shape), _const_spec((1, d))),
        (tile(d), tile(d)),
        (jax.ShapeDtypeStruct((t, d), F32), jax.ShapeDtypeStruct((t, d), BF16)),
        (pltpu.VMEM((n_dil * ATT_HPG * ROW_TILE, ATT_DH), F32), pltpu.VMEM((n_dil * ROW_TILE, LANES), F32)))


def _ffn_kernel(x_ref, h_ref, wu_ref, cw_ref, cb_ref, wd_ref, nw_ref, xo_ref, ho_ref, halo_ref):
    rows = x_ref.shape[0]
    hid = wd_ref.shape[0]

    h = h_ref[...]
    gate = _dot(h, wu_ref[:, :hid])
    val = _dot(h, wu_ref[:, hid:])
    halo = halo_ref[...]
    halo_ref[...] = gate[rows - SUBLANES:, :]
    cw = cw_ref[...]
    u = gate * cw[FFN_CONV - 1:FFN_CONV, :] + cb_ref[...]
    for j in range(1, FFN_CONV):
        u = u + _shift_rows(gate, j, halo) * cw[FFN_CONV - 1 - j:FFN_CONV - j, :]
    act = (_silu_of_half(u) * val).astype(BF16)
    x_new = x_ref[...] + _dot(act, wd_ref[...])
    xo_ref[...] = x_new
    ho_ref[...] = _rmsnorm_rows(x_new, nw_ref[...]).astype(ho_ref.dtype)


def _conv_ffn(x2d, h, w_up, conv_w, conv_b, w_down, norm_w, out_dtype):
    t, d = x2d.shape
    hid = w_down.shape[0]
    tile = pl.BlockSpec((ROW_TILE, d), lambda i: (i, 0))
    return _Call(
        "mxu", "conv_ffn", _ffn_kernel,
        (x2d, h, w_up, 0.5 * conv_w.astype(F32), 0.5 * conv_b.astype(F32).reshape(1, hid), w_down,
         norm_w.astype(F32).reshape(1, d)),
        (tile, tile, _const_spec((d, 2 * hid)), _const_spec((FFN_CONV, hid)), _const_spec((1, hid)),
         _const_spec((hid, d)), _const_spec((1, d))),
        (tile, tile),
        (jax.ShapeDtypeStruct((t, d), F32), jax.ShapeDtypeStruct((t, d), out_dtype)),
        (pltpu.VMEM((SUBLANES, hid), F32),), zero_scratch=True)


def _in_proj_pieces(n_cols):
    widths = (SSD_INNER, SSD_CONV_DIM, SSD_HEADS, GLA_KEY, GLA_KEY, GLA_VAL, GLA_RANK, GLA_VAL,
              3 * ATT_HEADS * ATT_DH)
    starts = np.concatenate([[0], np.cumsum(widths)]).tolist()
    z, xbc, dt, gq, gk, gv, glr, gr, qkv, gates = [
        (starts[i], (widths + (n_cols - starts[-1],))[i]) for i in range(len(widths) + 1)]
    assert gates[1] + 2 * GLA_KEY + 2 * GLA_VAL == W_BLOCK and qkv[1] == W_BLOCK
    return [gq + (GLA_DK ** -0.5,), gk + (1.0,), gv + (1.0,), gr + (0.5,), gates + (0.5,), qkv + (1.0,),
            xbc + (1.0,), z + (0.5,), dt + (1.0,), glr + (1.0,)]


def _layout_kernel(w_ref, o_ref):
    rows = w_ref.shape[0]
    pieces = _in_proj_pieces(w_ref.shape[1])
    dst = 0
    for src, width, scale in pieces[:-2]:
        val = w_ref[:, src:src + width]
        o_ref[:, dst:dst + width] = (val if scale == 1.0 else val * scale).astype(o_ref.dtype)
        dst += width
    (dt_src, dt_w, _), (lr_src, lr_w, _) = pieces[-2:]
    small = jnp.concatenate([w_ref[:, dt_src:dt_src + dt_w], w_ref[:, lr_src:lr_src + lr_w],
                             jnp.zeros((rows, SMALL_COLS - dt_w - lr_w), F32)], axis=1)
    o_ref[:, dst:dst + SMALL_COLS] = small.astype(o_ref.dtype)
    dst += SMALL_COLS
    o_ref[:, dst:] = jnp.zeros((rows, o_ref.shape[1] - dst), o_ref.dtype)


def _layout_in_proj(w_in, layer):
    _, d, n = w_in.shape
    rows = LANES
    return pl.pallas_call(
        _layout_kernel,
        grid=(d // rows,),
        in_specs=[pl.BlockSpec((None, rows, n), lambda i: (layer, i, 0))],
        out_specs=pl.BlockSpec((rows, 3 * W_BLOCK), lambda i: (i, 0)),
        out_shape=jax.ShapeDtypeStruct((d, 3 * W_BLOCK), BF16),
        compiler_params=_params(("parallel",)),
        name="layout_in_proj",
    )(w_in)


def _stream(x2d, tile0, seq, rope, layers, norm1_w, final_norm_w):
    (h,) = yield _first_norm(x2d, norm1_w[0], tile0, seq)
    x_s = x2d
    depth = len(layers)
    for l, p in enumerate(layers):
        xc, zs, small = yield _ssd_project(h, p["w_ssd"], p["ssd_conv_w"], p["ssd_conv_b"])
        gla_in, gates = yield _gla_gates_project(h, p["w_gla_gates"])
        att_in = yield _att_project(h, p["w_att"], rope)
        (y_ssd,) = yield _ssd_mixer(xc, zs, small, p["ssd_dt_bias"], p["ssd_a_log"], p["ssd_d"], p["ssd_norm_w"])
        (y_gla,) = yield _gla_mixer(gla_in, small, p["gla_gate_w"], p["gla_gate_b"], p["gla_norm_w"])
        att = []
        for g in range(ATT_GROUPS):
            att.append((yield _dilated_attention(att_in[g], att_in[ATT_GROUPS + g], att_in[2 * ATT_GROUPS + g], g)))
        x_s, h = yield _merge(x_s, y_ssd, y_gla, [a[0] for a in att], [a[1] for a in att], gates,
                              p["w_branch"], p["w_out"], p["norm2_w"], tile0 if l == 0 else 0)
        last = l == depth - 1
        x_s, h = yield _conv_ffn(x_s, h, p["ffn_up"], p["ffn_conv_w"], p["ffn_conv_b"], p["ffn_down"],
                                 final_norm_w if last else norm1_w[l + 1], F32 if last else BF16)
    return h


def kernel(x, norm1_w, w_in, ssd_conv_w, ssd_conv_b, ssd_dt_bias, ssd_a_log, ssd_d, ssd_norm_w,
           gla_gate_w, gla_gate_b, gla_norm_w, w_branch, w_out, norm2_w, ffn_up, ffn_conv_w,
           ffn_conv_b, ffn_down, final_norm_w):
    bsz, seq, d = x.shape
    depth = w_in.shape[0]
    assert seq % (ROW_TILE * max(ATT_DILATIONS)) == 0
    assert ROW_TILE % SSD_CHUNK == 0 and ROW_TILE % GLA_CHUNK == 0
    rope = _rope_table(seq)
    layers = []
    for l in range(depth):
        w = _layout_in_proj(w_in, l)
        layers.append({
            "w_ssd": w, "w_gla_gates": w, "w_att": w,
            "ssd_conv_w": ssd_conv_w[l], "ssd_conv_b": ssd_conv_b[l], "ssd_dt_bias": ssd_dt_bias[l],
            "ssd_a_log": ssd_a_log[l], "ssd_d": ssd_d[l], "ssd_norm_w": ssd_norm_w[l],
            "gla_gate_w": gla_gate_w[l], "gla_gate_b": gla_gate_b[l], "gla_norm_w": gla_norm_w[l],
            "w_branch": w_branch[l].astype(BF16), "w_out": w_out[l].astype(BF16), "norm2_w": norm2_w[l],
            "ffn_up": ffn_up[l].astype(BF16), "ffn_conv_w": ffn_conv_w[l], "ffn_conv_b": ffn_conv_b[l],
            "ffn_down": ffn_down[l].astype(BF16),
        })
    steps = seq // ROW_TILE
    x2d = x.reshape(bsz * seq, d)
    streams = [_stream(x2d, b * steps, seq, rope, layers, norm1_w, final_norm_w) for b in range(bsz)]
    return jnp.stack(_interleave(streams, steps))
```

```python
import functools
from typing import Callable, NamedTuple

import jax
import jax.numpy as jnp
import numpy as np
from jax import lax
from jax.experimental import pallas as pl
from jax.experimental.pallas import tpu as pltpu

F32 = jnp.float32
BF16 = jnp.bfloat16

LANES = 128
SUBLANES = 8
VMEM_LIMIT_BYTES = 56 * 1024 * 1024

RMS_EPS = 1e-6
SSD_HEADS = 16
SSD_HEAD_DIM = 64
SSD_INNER = SSD_HEADS * SSD_HEAD_DIM
SSD_STATE = 128
SSD_GROUPS = 4
SSD_CONV = 4
SSD_CONV_DIM = SSD_INNER + 2 * SSD_GROUPS * SSD_STATE
GLA_HEADS = 4
GLA_DK = 64
GLA_DV = 128
GLA_KEY = GLA_HEADS * GLA_DK
GLA_VAL = GLA_HEADS * GLA_DV
GLA_RANK = 16
GLA_TAU = 16.0
ATT_DILATIONS = (1, 4, 16)
ATT_WINDOWS = (128, 512, 2048)
ATT_GROUPS = 3
ATT_HPG = 4
ATT_HEADS = ATT_GROUPS * ATT_HPG
ATT_DH = 128
ATT_OUT = ATT_HPG * ATT_DH
ATT_BLOCK = 128
ROPE_THETA = 10000.0
FFN_CONV = 3

ROW_TILE = 512
PROJ_COLS = 512
SSD_CHUNK = 128
GLA_CHUNK = 128
SMALL_COLS = LANES
W_BLOCK = 3 * ATT_HEADS * ATT_DH
W_GLA_GATES_BLOCK, W_ATT_BLOCK, W_SSD_BLOCK = 0, 1, 2

NEG_INF = float("-inf")


def _params(sem):
    return pltpu.CompilerParams(dimension_semantics=sem, vmem_limit_bytes=VMEM_LIMIT_BYTES)


def _const_spec(shape, col_block=0):
    index = (0,) * (len(shape) - 1) + (col_block,)
    return pl.BlockSpec(shape, lambda *_: index, pipeline_mode=pl.Buffered(1))


class _Call(NamedTuple):
    kind: str
    name: str
    body: Callable
    args: tuple
    in_specs: tuple
    out_specs: tuple
    out_shape: tuple
    scratch_shapes: tuple = ()
    zero_scratch: bool = False


def _run(calls, steps):
    n_in = [len(c.args) for c in calls]
    n_out = [len(c.out_shape) for c in calls]
    n_scr = [len(c.scratch_shapes) for c in calls]

    def body(*refs):
        ins = refs[:sum(n_in)]
        outs = refs[sum(n_in):sum(n_in) + sum(n_out)]
        scrs = refs[sum(n_in) + sum(n_out):]

        @pl.when(pl.program_id(0) == 0)
        def _():
            s = 0
            for c, ns in zip(calls, n_scr):
                if c.zero_scratch:
                    for ref in scrs[s:s + ns]:
                        ref[...] = jnp.zeros_like(ref)
                s += ns

        i = o = s = 0
        for c, ni, no, ns in zip(calls, n_in, n_out, n_scr):
            c.body(*ins[i:i + ni], *outs[o:o + no], *scrs[s:s + ns])
            i, o, s = i + ni, o + no, s + ns

    flat = pl.pallas_call(
        body,
        grid=(steps,),
        in_specs=[sp for c in calls for sp in c.in_specs],
        out_specs=[sp for c in calls for sp in c.out_specs],
        out_shape=[sh for c in calls for sh in c.out_shape],
        scratch_shapes=[sc for c in calls for sc in c.scratch_shapes],
        compiler_params=_params(("arbitrary",)),
        name="__".join(c.name for c in calls),
    )(*[a for c in calls for a in c.args])
    outs, o = [], 0
    for no in n_out:
        outs.append(list(flat[o:o + no]))
        o += no
    return outs


def _interleave(streams, steps):
    pending = [next(st) for st in streams]
    done = [None] * len(streams)

    def advance(i, outs):
        try:
            pending[i] = streams[i].send(outs)
        except StopIteration as stop:
            pending[i], done[i] = None, stop.value

    while any(p is not None for p in pending):
        live = [i for i, p in enumerate(pending) if p is not None]
        group = live[:1]
        for i in live[1:]:
            if pending[i].kind != pending[group[0]].kind:
                group.append(i)
                break
        outs = _run([pending[i] for i in group], steps)
        for i, o in zip(group, outs):
            advance(i, o)
    return done


def _sigmoid_of_half(u):
    return 0.5 * jnp.tanh(u) + 0.5


def _silu_of_half(u):
    return u * jnp.tanh(u) + u


def _softplus(x):
    return jnp.maximum(x, 0.0) + jnp.log(1.0 + jnp.exp(-jnp.abs(x)))


def _dot(a, b):
    return jnp.dot(a, b, preferred_element_type=F32)


def _dot_nt(a, b):
    return lax.dot_general(a, b, (((1,), (1,)), ((), ())), preferred_element_type=F32)


def _dot_tn(a, b):
    return lax.dot_general(a, b, (((0,), (0,)), ((), ())), preferred_element_type=F32)


def _shift_rows(x, j, halo):
    n, c = x.shape
    x3 = x.reshape(n // SUBLANES, SUBLANES, c)
    prev = jnp.concatenate([halo[None], x3[:-1]], axis=0)
    sub = lax.broadcasted_iota(jnp.int32, x3.shape, 1)
    return pltpu.roll(jnp.where(sub >= SUBLANES - j, prev, x3), j, 1).reshape(n, c)


def _cumsum_rows(x):
    n = x.shape[0]
    row = lax.broadcasted_iota(jnp.int32, x.shape, 0)
    k = 1
    while k < n:
        x = x + jnp.where(row >= k, pltpu.roll(x, k, 0), 0.0)
        k *= 2
    return x


def _rmsnorm_rows(x, w):
    return x * lax.rsqrt(jnp.mean(x * x, axis=-1, keepdims=True) + RMS_EPS) * w


def _rope_table_kernel(inv_ref, cos_ref, sin_ref):
    rows = cos_ref.shape[0]
    base = pl.program_id(0) * rows
    pos = (base + lax.broadcasted_iota(jnp.int32, (rows, ATT_DH), 0)).astype(F32)
    lane = lax.broadcasted_iota(jnp.int32, (rows, ATT_DH), 1)
    ang = pos * inv_ref[...]
    cos_ref[...] = jnp.cos(ang)
    s = jnp.sin(ang)
    sin_ref[...] = jnp.where(lane < ATT_DH // 2, -s, s)


def _rope_table(seq):
    half = ATT_DH // 2
    inv = ROPE_THETA ** (-jnp.arange(half, dtype=F32) / half)
    inv = jnp.concatenate([inv, inv]).reshape(1, ATT_DH)
    rows = min(seq, 1024)
    return pl.pallas_call(
        _rope_table_kernel,
        grid=(seq // rows,),
        in_specs=[pl.BlockSpec((1, ATT_DH), lambda i: (0, 0))],
        out_specs=[pl.BlockSpec((rows, ATT_DH), lambda i: (i, 0))] * 2,
        out_shape=[jax.ShapeDtypeStruct((seq, ATT_DH), F32)] * 2,
        compiler_params=_params(("parallel",)),
        name="rope_table",
    )(inv)


def _norm_kernel(x_ref, w_ref, h_ref):
    h_ref[...] = _rmsnorm_rows(x_ref[...], w_ref[...]).astype(h_ref.dtype)


def _first_norm(x2d, w, tile0, seq):
    d = x2d.shape[1]
    return _Call("mxu", "first_norm", _norm_kernel, (x2d, w.astype(F32).reshape(1, d)),
                 (pl.BlockSpec((ROW_TILE, d), lambda i: (i + tile0, 0)), _const_spec((1, d))),
                 (pl.BlockSpec((ROW_TILE, d), lambda i: (i, 0)),), (jax.ShapeDtypeStruct((seq, d), BF16),))


def _ssd_proj_kernel(h_ref, w_ref, cw_ref, cb_ref, xc_ref, zs_ref, sm_ref, halo_ref):
    rows = h_ref.shape[0]

    h = h_ref[...]
    for j in range(SSD_CONV_DIM // PROJ_COLS):
        cols = slice(j * PROJ_COLS, (j + 1) * PROJ_COLS)
        acc = _dot(h, w_ref[:, cols])
        halo = halo_ref[:, cols]
        halo_ref[:, cols] = acc[rows - SUBLANES:, :]
        cw = cw_ref[:, cols]
        u = acc * cw[SSD_CONV - 1:SSD_CONV, :] + cb_ref[:, cols]
        for s in range(1, SSD_CONV):
            u = u + _shift_rows(acc, s, halo) * cw[SSD_CONV - 1 - s:SSD_CONV - s, :]
        xc_ref[:, cols] = _silu_of_half(u).astype(xc_ref.dtype)
    for j in range(SSD_INNER // PROJ_COLS):
        cols = slice(j * PROJ_COLS, (j + 1) * PROJ_COLS)
        wcols = slice(SSD_CONV_DIM + j * PROJ_COLS, SSD_CONV_DIM + (j + 1) * PROJ_COLS)
        zs_ref[:, cols] = _silu_of_half(_dot(h, w_ref[:, wcols])).astype(zs_ref.dtype)
    sm_ref[...] = _dot(h, w_ref[:, SSD_CONV_DIM + SSD_INNER:SSD_CONV_DIM + SSD_INNER + SMALL_COLS])


def _ssd_project(h, w, conv_w, conv_b):
    t, d = h.shape
    row = lambda i: (i, 0)
    return _Call(
        "mxu", "proj_ssd", _ssd_proj_kernel,
        (h, w, 0.5 * conv_w.astype(F32), 0.5 * conv_b.astype(F32).reshape(1, -1)),
        (pl.BlockSpec((ROW_TILE, d), row), _const_spec((d, W_BLOCK), W_SSD_BLOCK),
         _const_spec((SSD_CONV, SSD_CONV_DIM)), _const_spec((1, SSD_CONV_DIM))),
        (pl.BlockSpec((ROW_TILE, SSD_CONV_DIM), row), pl.BlockSpec((ROW_TILE, SSD_INNER), row),
         pl.BlockSpec((ROW_TILE, SMALL_COLS), row)),
        (jax.ShapeDtypeStruct((t, SSD_CONV_DIM), BF16), jax.ShapeDtypeStruct((t, SSD_INNER), BF16),
         jax.ShapeDtypeStruct((t, SMALL_COLS), F32)),
        (pltpu.VMEM((SUBLANES, SSD_CONV_DIM), F32),), zero_scratch=True)


def _gla_gates_proj_kernel(h_ref, w_ref, gla_ref, gt_ref):
    h = h_ref[...]
    n_gla = gla_ref.shape[1]
    for j in range(w_ref.shape[1] // PROJ_COLS):
        c0 = j * PROJ_COLS
        acc = _dot(h, w_ref[:, c0:c0 + PROJ_COLS])
        if c0 >= n_gla:
            gt_ref[:, c0 - n_gla:c0 - n_gla + PROJ_COLS] = _sigmoid_of_half(acc).astype(gt_ref.dtype)
        else:
            if c0 >= 2 * GLA_KEY + GLA_VAL:
                acc = _silu_of_half(acc)
            gla_ref[:, c0:c0 + PROJ_COLS] = acc.astype(gla_ref.dtype)


def _gla_gates_project(h, w):
    t, d = h.shape
    n_gla = 2 * GLA_KEY + 2 * GLA_VAL
    n_gates = W_BLOCK - n_gla
    row = lambda i: (i, 0)
    return _Call(
        "mxu", "proj_gla_gates", _gla_gates_proj_kernel, (h, w),
        (pl.BlockSpec((ROW_TILE, d), row), _const_spec((d, W_BLOCK), W_GLA_GATES_BLOCK)),
        (pl.BlockSpec((ROW_TILE, n_gla), row), pl.BlockSpec((ROW_TILE, n_gates), row)),
        (jax.ShapeDtypeStruct((t, n_gla), BF16), jax.ShapeDtypeStruct((t, n_gates), BF16)))


def _att_proj_kernel(h_ref, w_ref, cos_ref, sin_ref, *rest):
    out_refs, scr_ref = rest[:-1], rest[-1]
    h = h_ref[...]
    rows = h.shape[0]
    cos = cos_ref[...]
    sin = sin_ref[...]
    slot = 0
    for kind in range(3):
        for g, d in enumerate(ATT_DILATIONS):
            c0 = (kind * ATT_GROUPS + g) * ATT_OUT
            acc = _dot(h, w_ref[:, c0:c0 + ATT_OUT])
            o_ref = out_refs[kind * ATT_GROUPS + g]
            for a in range(ATT_HPG):
                xh = acc[:, a * ATT_DH:(a + 1) * ATT_DH]
                if kind < 2:
                    xh = xh * cos + pltpu.roll(xh, ATT_DH // 2, 1) * sin
                if kind == 0:
                    xh = xh * (ATT_DH ** -0.5)
                if d == 1:
                    o_ref[:, a * ATT_DH:(a + 1) * ATT_DH] = xh.astype(o_ref.dtype)
                    continue
                scr_ref[slot * rows:(slot + 1) * rows, :] = xh
                for r in range(d):
                    c = r * ATT_OUT + a * ATT_DH
                    o_ref[:, c:c + ATT_DH] = (
                        scr_ref[pl.ds(slot * rows + r, rows // d, stride=d), :].astype(o_ref.dtype))
                slot += 1


def _att_project(h, w_qkv, rope):
    t, dm = h.shape
    row = lambda i: (i, 0)
    rope_spec = pl.BlockSpec((ROW_TILE, ATT_DH), row)
    out_specs, out_shape = [], []
    for _ in range(3):
        for d in ATT_DILATIONS:
            out_specs.append(pl.BlockSpec((ROW_TILE // d, d * ATT_OUT), row))
            out_shape.append(jax.ShapeDtypeStruct((t // d, d * ATT_OUT), BF16))
    n_slots = 3 * sum(d > 1 for d in ATT_DILATIONS) * ATT_HPG
    return _Call(
        "mxu", "proj_att", _att_proj_kernel, (h, w_qkv, *rope),
        (pl.BlockSpec((ROW_TILE, dm), row), _const_spec((dm, W_BLOCK), W_ATT_BLOCK), rope_spec, rope_spec),
        tuple(out_specs), tuple(out_shape),
        (pltpu.VMEM((n_slots * ROW_TILE, ATT_DH), F32),))


def _ssd_kernel(xc_ref, zs_ref, sm_ref, dtb_ref, aneg_ref, dsk_ref, nw_ref, y_ref, st_ref):
    L = SSD_CHUNK
    pair = 2 * SSD_HEAD_DIM

    ti = lax.broadcasted_iota(jnp.int32, (L, L), 0)
    si = lax.broadcasted_iota(jnp.int32, (L, L), 1)
    causal = ti >= si
    low = lax.broadcasted_iota(jnp.int32, (L, pair), 1) < SSD_HEAD_DIM
    low_row = lax.broadcasted_iota(jnp.int32, (1, pair), 1) < SSD_HEAD_DIM
    for sub in range(xc_ref.shape[0] // L):
        rows = slice(sub * L, (sub + 1) * L)
        _ssd_chunk(xc_ref, zs_ref, sm_ref, dtb_ref, aneg_ref, dsk_ref, nw_ref, y_ref, st_ref, rows,
                   causal, low, low_row)


def _ssd_chunk(xc_ref, zs_ref, sm_ref, dtb_ref, aneg_ref, dsk_ref, nw_ref, y_ref, st_ref, rows,
               causal, low, low_row):
    L = SSD_CHUNK
    n_state = SSD_STATE
    pair = 2 * SSD_HEAD_DIM
    hpg = SSD_HEADS // SSD_GROUPS
    dt = _softplus(sm_ref[rows, :] + dtb_ref[...])
    a_cum = _cumsum_rows(dt * aneg_ref[...])
    a_last = a_cum[L - 1:L, :]
    w_end = jnp.exp(a_last - a_cum) * dt
    e_cum = jnp.exp(a_cum)
    chunk_decay = jnp.exp(a_last)
    a_cum_t = a_cum.T
    dt_t = dt.T

    b_off = SSD_INNER
    c_off = SSD_INNER + SSD_GROUPS * n_state
    y_blocks = []
    for p in range(SSD_HEADS // 2):
        g = (2 * p) // hpg
        if (2 * p) % hpg == 0:
            b_g = xc_ref[rows, b_off + g * n_state:b_off + (g + 1) * n_state]
            c_bf = xc_ref[rows, c_off + g * n_state:c_off + (g + 1) * n_state]
            cb = _dot_nt(c_bf, b_g)
            c_g = c_bf.astype(F32)
        x_pair = xc_ref[rows, p * pair:(p + 1) * pair]
        st = st_ref[p]
        lhs = []
        for h in (2 * p, 2 * p + 1):
            seg = a_cum[:, h:h + 1] - a_cum_t[h:h + 1, :]
            dec = jnp.exp(jnp.where(causal, seg, NEG_INF))
            m = (cb * dec * dt_t[h:h + 1, :]).astype(BF16)
            c_s = (c_g * e_cum[:, h:h + 1]).astype(BF16)
            lhs.append(jnp.concatenate([m, c_s], axis=1))
        yy = _dot(jnp.concatenate(lhs, axis=0), jnp.concatenate([x_pair, st.astype(BF16)], axis=0))
        x_f = x_pair.astype(F32)
        y_blocks.append(jnp.where(low, yy[:L], yy[L:]) + dsk_ref[:, p * pair:(p + 1) * pair] * x_f)

        w_pair = jnp.where(low, w_end[:, 2 * p:2 * p + 1], w_end[:, 2 * p + 1:2 * p + 2])
        cd = jnp.where(low_row, chunk_decay[:, 2 * p:2 * p + 1], chunk_decay[:, 2 * p + 1:2 * p + 2])
        st_ref[p] = st * cd + _dot_tn(b_g, (x_f * w_pair).astype(BF16))

    y = jnp.concatenate(y_blocks, axis=1)
    y = y * zs_ref[rows, :].astype(F32)
    gw = SSD_INNER // SSD_GROUPS
    nw = nw_ref[...]
    for g in range(SSD_GROUPS):
        cols = slice(g * gw, (g + 1) * gw)
        y_ref[rows, cols] = _rmsnorm_rows(y[:, cols], nw[:, cols]).astype(y_ref.dtype)


def _ssd_mixer(xc, zs, small, dt_bias, a_log, d_skip, norm_w):
    t = xc.shape[0]
    pad = SMALL_COLS - SSD_HEADS
    dtb = jnp.pad(dt_bias.astype(F32), (0, pad)).reshape(1, SMALL_COLS)
    aneg = jnp.pad(-jnp.exp(a_log.astype(F32)), (0, pad)).reshape(1, SMALL_COLS)
    dsk = jnp.repeat(d_skip.astype(F32), SSD_HEAD_DIM).reshape(1, SSD_INNER)
    row = lambda i: (i, 0)
    return _Call(
        "vec", "ssd_mixer", _ssd_kernel,
        (xc, zs, small, dtb, aneg, dsk, norm_w.astype(F32).reshape(1, -1)),
        (pl.BlockSpec((ROW_TILE, SSD_CONV_DIM), row), pl.BlockSpec((ROW_TILE, SSD_INNER), row),
         pl.BlockSpec((ROW_TILE, SMALL_COLS), row), _const_spec((1, SMALL_COLS)),
         _const_spec((1, SMALL_COLS)), _const_spec((1, SSD_INNER)), _const_spec((1, SSD_INNER))),
        (pl.BlockSpec((ROW_TILE, SSD_INNER), row),),
        (jax.ShapeDtypeStruct((t, SSD_INNER), BF16),),
        (pltpu.VMEM((SSD_HEADS // 2, SSD_STATE, 2 * SSD_HEAD_DIM), F32),), zero_scratch=True)


def _gla_level_ref_rows(b_ref, m, C):
    blk = 2 * m
    width = b_ref.shape[1]
    if blk >= SUBLANES:
        parts = []
        for s in range(0, C, blk):
            row = b_ref[s + m - 1:s + m, :]
            parts.append(jnp.broadcast_to(row, (blk, width)))
        return parts[0] if len(parts) == 1 else jnp.concatenate(parts, axis=0)
    b = b_ref[...]
    off = lax.broadcasted_iota(jnp.int32, (C, width), 0) % blk
    out = b
    for o in range(blk):
        delta = o - (m - 1)
        if delta == 0:
            continue
        out = jnp.where(off == o, pltpu.roll(b, delta % C, 0), out)
    return out


def _gla_kernel(qk_ref, v_ref, rs_ref, sm_ref, gw_ref, gb_ref, nw_ref, y_ref, st_ref, b_ref):
    C = GLA_CHUNK

    for sub in range(qk_ref.shape[0] // C):
        _gla_chunk(qk_ref, v_ref, rs_ref, sm_ref, gw_ref, gb_ref, nw_ref, y_ref, st_ref, b_ref.at[sub],
                   slice(sub * C, (sub + 1) * C))


def _gla_chunk(qk_ref, v_ref, rs_ref, sm_ref, gw_ref, gb_ref, nw_ref, y_ref, st_ref, b_ref, rows):
    C = GLA_CHUNK
    pair = 2 * GLA_DK
    qk = qk_ref[rows, :].astype(F32)
    q = qk[:, :GLA_KEY]
    k = qk[:, GLA_KEY:]

    s_f = sm_ref[rows, :]
    s_hi = s_f.astype(BF16)
    s_lo = (s_f - s_hi.astype(F32)).astype(BF16)
    g_f = gw_ref[...]
    g_hi = g_f.astype(BF16)
    g_lo = (g_f - g_hi.astype(F32)).astype(BF16)
    pre = _dot(s_hi, g_hi) + _dot(s_hi, g_lo) + _dot(s_lo, g_hi) + gb_ref[...]
    log_a = -_softplus(-pre) * (1.0 / GLA_TAU)
    b = _cumsum_rows(log_a)
    b_ref[...] = b
    b_last = b[C - 1:C, :]

    rowi = lax.broadcasted_iota(jnp.int32, (C, GLA_KEY), 0)
    lane = lax.broadcasted_iota(jnp.int32, (C, pair), 1)
    head_lanes = (lane < GLA_DK, lane >= GLA_DK)
    ti = lax.broadcasted_iota(jnp.int32, (C, C), 0)
    si = lax.broadcasted_iota(jnp.int32, (C, C), 1)
    ts_xor = ti ^ si

    head_mask = [jnp.where(sel, 1.0, 0.0).astype(BF16) for sel in head_lanes]

    def head_ops(qa, ka, h, mask_k=False):
        cols = slice((h // 2) * pair, (h // 2 + 1) * pair)
        qh = qa[:, cols] * head_mask[h % 2]
        kh = ka[:, cols] * head_mask[h % 2] if mask_k else ka[:, cols]
        return qh, kh

    def pair_scores(qa, ka, j):
        cols = slice(j * pair, (j + 1) * pair)
        qs = jnp.concatenate([qa[:, cols] * head_mask[0], qa[:, cols] * head_mask[1]], axis=0)
        return _dot_nt(qs, ka[:, cols])

    scores = []
    q_bf, k_bf = q.astype(BF16), k.astype(BF16)
    for j in range(GLA_HEADS // 2):
        p = pair_scores(q_bf, k_bf, j)
        scores += [jnp.where(ti == si, p[:C], 0.0), jnp.where(ti == si, p[C:], 0.0)]
    m = C // 2
    while m >= 1:
        blk = 2 * m
        upper = (rowi & m) != 0
        beta = _gla_level_ref_rows(b_ref, m, C)
        e = jnp.exp(jnp.where(upper, b - beta, beta - b))
        qt = jnp.where(upper, q * e, 0.0).astype(BF16)
        kt = jnp.where(upper, 0.0, k * e).astype(BF16)
        for j in range(GLA_HEADS // 2):
            p = pair_scores(qt, kt, j)
            for i, ph in enumerate((p[:C], p[C:])):
                if blk < C:
                    ph = jnp.where(ts_xor < blk, ph, 0.0)
                scores[2 * j + i] = scores[2 * j + i] + ph
        m //= 2

    qe = (q * jnp.exp(b)).astype(BF16)
    k_end = (k * jnp.exp(b_last - b)).astype(BF16)
    st_decay = jnp.exp(b_last)
    nw = nw_ref[...]
    for h in range(GLA_HEADS):
        j = h // 2
        vh = v_ref[rows, h * GLA_DV:(h + 1) * GLA_DV]
        st = st_ref[h]
        qh, kh = head_ops(qe, k_end, h, mask_k=True)
        o = _dot(scores[h].astype(BF16), vh) + _dot_nt(qh, st.astype(BF16))
        st_ref[h] = st * st_decay[:, j * pair:(j + 1) * pair] + _dot_tn(vh, kh)
        o = _rmsnorm_rows(o, nw) * rs_ref[rows, h * GLA_DV:(h + 1) * GLA_DV].astype(F32)
        y_ref[rows, h * GLA_DV:(h + 1) * GLA_DV] = o.astype(y_ref.dtype)


def _gla_mixer(gla_in, small, gate_w, gate_b, norm_w):
    t = gla_in.shape[0]
    gw = jnp.zeros((SMALL_COLS, GLA_KEY), F32).at[SSD_HEADS:SSD_HEADS + GLA_RANK].set(gate_w.astype(F32))
    row = lambda col: (lambda i: (i, col))
    return _Call(
        "vec", "gla_mixer", _gla_kernel,
        (gla_in, gla_in, gla_in, small, gw, gate_b.astype(F32).reshape(1, -1), norm_w.astype(F32).reshape(1, -1)),
        (pl.BlockSpec((ROW_TILE, 2 * GLA_KEY), row(0)), pl.BlockSpec((ROW_TILE, GLA_VAL), row(1)),
         pl.BlockSpec((ROW_TILE, GLA_VAL), row(2)), pl.BlockSpec((ROW_TILE, SMALL_COLS), row(0)),
         _const_spec((SMALL_COLS, GLA_KEY)), _const_spec((1, GLA_KEY)), _const_spec((1, GLA_DV))),
        (pl.BlockSpec((ROW_TILE, GLA_VAL), row(0)),),
        (jax.ShapeDtypeStruct((t, GLA_VAL), BF16),),
        (pltpu.VMEM((GLA_HEADS, GLA_DV, 2 * GLA_DK), F32),
         pltpu.VMEM((ROW_TILE // GLA_CHUNK, GLA_CHUNK, GLA_KEY), F32)),
        zero_scratch=True)


def _att_kernel(q_ref, kh_ref, k_ref, vh_ref, v_ref, o_ref, lse_ref, ktbuf_ref, vbuf_ref, *, nb):
    R = q_ref.shape[0]
    blk = ATT_BLOCK
    for h in range(ATT_HPG):
        cols = slice(h * ATT_DH, (h + 1) * ATT_DH)
        ktbuf_ref[cols, :blk] = kh_ref[:, cols].T
        ktbuf_ref[cols, blk:] = k_ref[:, cols].T
    vbuf_ref[:blk, :] = vh_ref[...]
    vbuf_ref[blk:, :] = v_ref[...]
    qi = lax.broadcasted_iota(jnp.int32, (blk, 2 * blk), 0)
    kj = lax.broadcasted_iota(jnp.int32, (blk, 2 * blk), 1)
    band = jnp.abs(2 * (kj - qi) - blk) <= blk
    first_key = jnp.where(pl.program_id(0) % nb > 0, 0, blk)
    lane = lax.broadcasted_iota(jnp.int32, (blk, LANES), 1)
    for sb in range(R // blk):
        rows = slice(sb * blk, (sb + 1) * blk)
        win = slice(sb * blk, (sb + 2) * blk)
        lse_tile = jnp.zeros((blk, LANES), F32)
        for h in range(ATT_HPG):
            cols = slice(h * ATT_DH, (h + 1) * ATT_DH)
            s = _dot(q_ref[rows, cols], ktbuf_ref[cols, win])
            s = jnp.where(band, s, NEG_INF)
            if sb == 0:
                s = jnp.where(kj >= first_key, s, NEG_INF)
            mx = jnp.max(s, axis=-1, keepdims=True)
            p = jnp.exp(s - mx)
            den = jnp.sum(p, axis=-1, keepdims=True)
            o = _dot(p.astype(BF16), vbuf_ref[win, cols])
            o_ref[rows, cols] = (o * (1.0 / den)).astype(o_ref.dtype)
            lse_tile = jnp.where(lane == h, mx + jnp.log(den), lse_tile)
        lse_ref[rows, :] = lse_tile


def _dilated_attention(q, k, v, g):
    d = ATT_DILATIONS[g]
    sub = q.shape[0]
    assert ATT_WINDOWS[g] // d == ATT_BLOCK and sub % ROW_TILE == 0
    nb = sub // ROW_TILE
    per = ROW_TILE // ATT_BLOCK
    main = pl.BlockSpec((ROW_TILE, ATT_OUT), lambda i: (i % nb, i // nb))
    halo = pl.BlockSpec((ATT_BLOCK, ATT_OUT), lambda i: (jnp.maximum((i % nb) * per - 1, 0), i // nb))
    return _Call(
        "vec", f"dilated_attention_d{d}", functools.partial(_att_kernel, nb=nb), (q, k, k, v, v),
        (main, halo, main, halo, main),
        (main, pl.BlockSpec((ROW_TILE, LANES), lambda i: (i % nb, i // nb))),
        (jax.ShapeDtypeStruct((sub, d * ATT_OUT), BF16), jax.ShapeDtypeStruct((sub, d * LANES), F32)),
        (pltpu.VMEM((ATT_OUT, ROW_TILE + ATT_BLOCK), BF16), pltpu.VMEM((ROW_TILE + ATT_BLOCK, ATT_OUT), BF16)))


def _interleave_rows(src_ref, scr_ref, d, rows, n_blk, base):
    if d == 1:
        return [src_ref[:, a * LANES:(a + 1) * LANES].astype(F32) for a in range(n_blk)]
    for r in range(d):
        for a in range(n_blk):
            c = (r * n_blk + a) * LANES
            scr_ref[pl.ds(base + a * rows + r, rows // d, stride=d), :] = src_ref[:, c:c + LANES].astype(F32)
    return [scr_ref[base + a * rows:base + (a + 1) * rows, :] for a in range(n_blk)]


def _merge_kernel(x_ref, ys_ref, yg_ref, o0_ref, o1_ref, o2_ref, l0_ref, l1_ref, l2_ref, gt_ref,
                  wb_ref, wo_ref, nw_ref, xo_ref, h_ref, oscr_ref, lscr_ref):
    rows = x_ref.shape[0]
    d = x_ref.shape[1]
    slots = [sum(dd > 1 for dd in ATT_DILATIONS[:g]) for g in range(ATT_GROUPS)]
    lses = [_interleave_rows(l_ref, lscr_ref, ATT_DILATIONS[g], rows, 1, slots[g] * rows)[0]
            for g, l_ref in enumerate((l0_ref, l1_ref, l2_ref))]
    mx = jnp.maximum(jnp.maximum(lses[0], lses[1]), lses[2])
    es = [jnp.exp(l - mx) for l in lses]
    inv = 1.0 / (es[0] + es[1] + es[2])
    acc = [jnp.zeros((rows, ATT_DH), F32) for _ in range(ATT_HPG)]
    for g, o_ref in enumerate((o0_ref, o1_ref, o2_ref)):
        o_g = _interleave_rows(o_ref, oscr_ref, ATT_DILATIONS[g], rows, ATT_HPG, slots[g] * ATT_HPG * rows)
        w_g = es[g] * inv
        for h in range(ATT_HPG):
            acc[h] = acc[h] + w_g[:, h:h + 1] * o_g[h]
    y_att = jnp.concatenate([a.astype(BF16) for a in acc], axis=1)

    merged = _sig_gate(gt_ref, 0, d) * _dot(ys_ref[...], wb_ref[:SSD_INNER, :])
    merged = merged + _sig_gate(gt_ref, 1, d) * _dot(yg_ref[...], wb_ref[SSD_INNER:SSD_INNER + GLA_VAL, :])
    merged = merged + _sig_gate(gt_ref, 2, d) * _dot(y_att, wb_ref[SSD_INNER + GLA_VAL:, :])
    x_new = x_ref[...] + _dot(merged.astype(BF16), wo_ref[...])
    xo_ref[...] = x_new
    h_ref[...] = _rmsnorm_rows(x_new, nw_ref[...]).astype(h_ref.dtype)


def _sig_gate(gt_ref, i, d):
    return gt_ref[:, i * d:(i + 1) * d].astype(F32)


def _merge(x2d, y_ssd, y_gla, att_o, att_lse, gates, w_branch, w_out, norm_w, tile0=0):
    t, d = y_ssd.shape[0], x2d.shape[1]
    row = lambda i: (i, 0)
    tile = lambda n: pl.BlockSpec((ROW_TILE, n), row)
    dil_tile = lambda n: [pl.BlockSpec((ROW_TILE // dil, dil * n), row) for dil in ATT_DILATIONS]
    n_dil = sum(dil > 1 for dil in ATT_DILATIONS)
    return _Call(
        "mxu", "merge_out", _merge_kernel,
        (x2d, y_ssd, y_gla, *att_o, *att_lse, gates, w_branch, w_out, norm_w.astype(F32).reshape(1, d)),
        (pl.BlockSpec((ROW_TILE, d), lambda i: (i + tile0, 0)), tile(SSD_INNER), tile(GLA_VAL),
         *dil_tile(ATT_OUT), *dil_tile(LANES), tile(3 * d),
         _const_spec(w_branch.shape), _const_spec(w_out.shape), _const_spec((1, d))),
        (tile(d), tile(d)),
        (jax.ShapeDtypeStruct((t, d), F32), jax.ShapeDtypeStruct((t, d), BF16)),
        (pltpu.VMEM((n_dil * ATT_HPG * ROW_TILE, ATT_DH), F32), pltpu.VMEM((n_dil * ROW_TILE, LANES), F32)))


def _ffn_kernel(x_ref, h_ref, wu_ref, cw_ref, cb_ref, wd_ref, nw_ref, xo_ref, ho_ref, halo_ref):
    rows = x_ref.shape[0]
    hid = wd_ref.shape[0]

    h = h_ref[...]
    gate = _dot(h, wu_ref[:, :hid])
    val = _dot(h, wu_ref[:, hid:])
    halo = halo_ref[...]
    halo_ref[...] = gate[rows - SUBLANES:, :]
    cw = cw_ref[...]
    u = gate * cw[FFN_CONV - 1:FFN_CONV, :] + cb_ref[...]
    for j in range(1, FFN_CONV):
        u = u + _shift_rows(gate, j, halo) * cw[FFN_CONV - 1 - j:FFN_CONV - j, :]
    act = (_silu_of_half(u) * val).astype(BF16)
    x_new = x_ref[...] + _dot(act, wd_ref[...])
    xo_ref[...] = x_new
    ho_ref[...] = _rmsnorm_rows(x_new, nw_ref[...]).astype(ho_ref.dtype)


def _conv_ffn(x2d, h, w_up, conv_w, conv_b, w_down, norm_w, out_dtype):
    t, d = x2d.shape
    hid = w_down.shape[0]
    tile = pl.BlockSpec((ROW_TILE, d), lambda i: (i, 0))
    return _Call(
        "mxu", "conv_ffn", _ffn_kernel,
        (x2d, h, w_up, 0.5 * conv_w.astype(F32), 0.5 * conv_b.astype(F32).reshape(1, hid), w_down,
         norm_w.astype(F32).reshape(1, d)),
        (tile, tile, _const_spec((d, 2 * hid)), _const_spec((FFN_CONV, hid)), _const_spec((1, hid)),
         _const_spec((hid, d)), _const_spec((1, d))),
        (tile, tile),
        (jax.ShapeDtypeStruct((t, d), F32), jax.ShapeDtypeStruct((t, d), out_dtype)),
        (pltpu.VMEM((SUBLANES, hid), F32),), zero_scratch=True)


def _in_proj_pieces(n_cols):
    widths = (SSD_INNER, SSD_CONV_DIM, SSD_HEADS, GLA_KEY, GLA_KEY, GLA_VAL, GLA_RANK, GLA_VAL,
              3 * ATT_HEADS * ATT_DH)
    starts = np.concatenate([[0], np.cumsum(widths)]).tolist()
    z, xbc, dt, gq, gk, gv, glr, gr, qkv, gates = [
        (starts[i], (widths + (n_cols - starts[-1],))[i]) for i in range(len(widths) + 1)]
    assert gates[1] + 2 * GLA_KEY + 2 * GLA_VAL == W_BLOCK and qkv[1] == W_BLOCK
    return [gq + (GLA_DK ** -0.5,), gk + (1.0,), gv + (1.0,), gr + (0.5,), gates + (0.5,), qkv + (1.0,),
            xbc + (1.0,), z + (0.5,), dt + (1.0,), glr + (1.0,)]


def _layout_kernel(w_ref, o_ref):
    rows = w_ref.shape[0]
    pieces = _in_proj_pieces(w_ref.shape[1])
    dst = 0
    for src, width, scale in pieces[:-2]:
        val = w_ref[:, src:src + width]
        o_ref[:, dst:dst + width] = (val if scale == 1.0 else val * scale).astype(o_ref.dtype)
        dst += width
    (dt_src, dt_w, _), (lr_src, lr_w, _) = pieces[-2:]
    small = jnp.concatenate([w_ref[:, dt_src:dt_src + dt_w], w_ref[:, lr_src:lr_src + lr_w],
                             jnp.zeros((rows, SMALL_COLS - dt_w - lr_w), F32)], axis=1)
    o_ref[:, dst:dst + SMALL_COLS] = small.astype(o_ref.dtype)
    dst += SMALL_COLS
    o_ref[:, dst:] = jnp.zeros((rows, o_ref.shape[1] - dst), o_ref.dtype)


def _layout_in_proj(w_in, layer):
    _, d, n = w_in.shape
    rows = LANES
    return pl.pallas_call(
        _layout_kernel,
        grid=(d // rows,),
        in_specs=[pl.BlockSpec((None, rows, n), lambda i: (layer, i, 0))],
        out_specs=pl.BlockSpec((rows, 3 * W_BLOCK), lambda i: (i, 0)),
        out_shape=jax.ShapeDtypeStruct((d, 3 * W_BLOCK), BF16),
        compiler_params=_params(("parallel",)),
        name="layout_in_proj",
    )(w_in)


def _stream(x2d, tile0, seq, rope, layers, norm1_w, final_norm_w):
    (h,) = yield _first_norm(x2d, norm1_w[0], tile0, seq)
    x_s = x2d
    depth = len(layers)
    for l, p in enumerate(layers):
        xc, zs, small = yield _ssd_project(h, p["w_ssd"], p["ssd_conv_w"], p["ssd_conv_b"])
        gla_in, gates = yield _gla_gates_project(h, p["w_gla_gates"])
        att_in = yield _att_project(h, p["w_att"], rope)
        (y_ssd,) = yield _ssd_mixer(xc, zs, small, p["ssd_dt_bias"], p["ssd_a_log"], p["ssd_d"], p["ssd_norm_w"])
        (y_gla,) = yield _gla_mixer(gla_in, small, p["gla_gate_w"], p["gla_gate_b"], p["gla_norm_w"])
        att = []
        for g in range(ATT_GROUPS):
            att.append((yield _dilated_attention(att_in[g], att_in[ATT_GROUPS + g], att_in[2 * ATT_GROUPS + g], g)))
        x_s, h = yield _merge(x_s, y_ssd, y_gla, [a[0] for a in att], [a[1] for a in att], gates,
                              p["w_branch"], p["w_out"], p["norm2_w"], tile0 if l == 0 else 0)
        last = l == depth - 1
        x_s, h = yield _conv_ffn(x_s, h, p["ffn_up"], p["ffn_conv_w"], p["ffn_conv_b"], p["ffn_down"],
                                 final_norm_w if last else norm1_w[l + 1], F32 if last else BF16)
    return h


def _stack_kernel(*refs):
    o_ref = refs[-1]
    for j, src_ref in enumerate(refs[:-1]):
        @pl.when(pl.program_id(0) == j)
        def _(src_ref=src_ref):
            o_ref[...] = src_ref[...]


def _stack_streams(outs):
    seq, d = outs[0].shape
    in_specs = [pl.BlockSpec((ROW_TILE, d), functools.partial(lambda j, b, i: (jnp.where(b == j, i, 0), 0), j))
                for j in range(len(outs))]
    return pl.pallas_call(
        _stack_kernel,
        grid=(len(outs), seq // ROW_TILE),
        in_specs=in_specs,
        out_specs=pl.BlockSpec((None, ROW_TILE, d), lambda b, i: (b, i, 0)),
        out_shape=jax.ShapeDtypeStruct((len(outs), seq, d), outs[0].dtype),
        compiler_params=_params(("parallel", "parallel")),
        name="stack_streams",
    )(*outs)


def kernel(x, norm1_w, w_in, ssd_conv_w, ssd_conv_b, ssd_dt_bias, ssd_a_log, ssd_d, ssd_norm_w,
           gla_gate_w, gla_gate_b, gla_norm_w, w_branch, w_out, norm2_w, ffn_up, ffn_conv_w,
           ffn_conv_b, ffn_down, final_norm_w):
    bsz, seq, d = x.shape
    depth = w_in.shape[0]
    assert seq % (ROW_TILE * max(ATT_DILATIONS)) == 0
    assert ROW_TILE % SSD_CHUNK == 0 and ROW_TILE % GLA_CHUNK == 0
    rope = _rope_table(seq)
    layers = []
    for l in range(depth):
        w = _layout_in_proj(w_in, l)
        layers.append({
            "w_ssd": w, "w_gla_gates": w, "w_att": w,
            "ssd_conv_w": ssd_conv_w[l], "ssd_conv_b": ssd_conv_b[l], "ssd_dt_bias": ssd_dt_bias[l],
            "ssd_a_log": ssd_a_log[l], "ssd_d": ssd_d[l], "ssd_norm_w": ssd_norm_w[l],
            "gla_gate_w": gla_gate_w[l], "gla_gate_b": gla_gate_b[l], "gla_norm_w": gla_norm_w[l],
            "w_branch": w_branch[l].astype(BF16), "w_out": w_out[l].astype(BF16), "norm2_w": norm2_w[l],
            "ffn_up": ffn_up[l].astype(BF16), "ffn_conv_w": ffn_conv_w[l], "ffn_conv_b": ffn_conv_b[l],
            "ffn_down": ffn_down[l].astype(BF16),
        })
    steps = seq // ROW_TILE
    x2d = x.reshape(bsz * seq, d)
    streams = [_stream(x2d, b * steps, seq, rope, layers, norm1_w, final_norm_w) for b in range(bsz)]
    return _stack_streams(_interleave(streams, steps))
```

```python
import functools
from typing import Callable, NamedTuple

import jax
import jax.numpy as jnp
import numpy as np
from jax import lax
from jax.experimental import pallas as pl
from jax.experimental.pallas import tpu as pltpu

F32 = jnp.float32
BF16 = jnp.bfloat16

LANES = 128
SUBLANES = 8
VMEM_LIMIT_BYTES = 56 * 1024 * 1024

RMS_EPS = 1e-6
SSD_HEADS = 16
SSD_HEAD_DIM = 64
SSD_INNER = SSD_HEADS * SSD_HEAD_DIM
SSD_STATE = 128
SSD_GROUPS = 4
SSD_CONV = 4
SSD_CONV_DIM = SSD_INNER + 2 * SSD_GROUPS * SSD_STATE
GLA_HEADS = 4
GLA_DK = 64
GLA_DV = 128
GLA_KEY = GLA_HEADS * GLA_DK
GLA_VAL = GLA_HEADS * GLA_DV
GLA_RANK = 16
GLA_TAU = 16.0
ATT_DILATIONS = (1, 4, 16)
ATT_WINDOWS = (128, 512, 2048)
ATT_GROUPS = 3
ATT_HPG = 4
ATT_HEADS = ATT_GROUPS * ATT_HPG
ATT_DH = 128
ATT_OUT = ATT_HPG * ATT_DH
ATT_BLOCK = 128
ROPE_THETA = 10000.0
FFN_CONV = 3

ROW_TILE = 512
PROJ_COLS = 512
SSD_CHUNK = 128
GLA_CHUNK = 128
SMALL_COLS = LANES
W_BLOCK = 3 * ATT_HEADS * ATT_DH
W_GLA_GATES_BLOCK, W_ATT_BLOCK, W_SSD_BLOCK = 0, 1, 2

NEG_INF = float("-inf")


def _params(sem):
    return pltpu.CompilerParams(dimension_semantics=sem, vmem_limit_bytes=VMEM_LIMIT_BYTES)


def _const_spec(shape, col_block=0):
    index = (0,) * (len(shape) - 1) + (col_block,)
    return pl.BlockSpec(shape, lambda *_: index, pipeline_mode=pl.Buffered(1))


class _Call(NamedTuple):
    kind: str
    name: str
    body: Callable
    args: tuple
    in_specs: tuple
    out_specs: tuple
    out_shape: tuple
    scratch_shapes: tuple = ()
    zero_scratch: bool = False


def _run(calls, steps):
    n_in = [len(c.args) for c in calls]
    n_out = [len(c.out_shape) for c in calls]
    n_scr = [len(c.scratch_shapes) for c in calls]

    def body(*refs):
        ins = refs[:sum(n_in)]
        outs = refs[sum(n_in):sum(n_in) + sum(n_out)]
        scrs = refs[sum(n_in) + sum(n_out):]

        @pl.when(pl.program_id(0) == 0)
        def _():
            s = 0
            for c, ns in zip(calls, n_scr):
                if c.zero_scratch:
                    for ref in scrs[s:s + ns]:
                        ref[...] = jnp.zeros_like(ref)
                s += ns

        i = o = s = 0
        for c, ni, no, ns in zip(calls, n_in, n_out, n_scr):
            c.body(*ins[i:i + ni], *outs[o:o + no], *scrs[s:s + ns])
            i, o, s = i + ni, o + no, s + ns

    flat = pl.pallas_call(
        body,
        grid=(steps,),
        in_specs=[sp for c in calls for sp in c.in_specs],
        out_specs=[sp for c in calls for sp in c.out_specs],
        out_shape=[sh for c in calls for sh in c.out_shape],
        scratch_shapes=[sc for c in calls for sc in c.scratch_shapes],
        compiler_params=_params(("arbitrary",)),
        name="__".join(c.name for c in calls),
    )(*[a for c in calls for a in c.args])
    outs, o = [], 0
    for no in n_out:
        outs.append(list(flat[o:o + no]))
        o += no
    return outs


def _interleave(streams, steps):
    pending = [next(st) for st in streams]
    done = [None] * len(streams)

    def advance(i, outs):
        try:
            pending[i] = streams[i].send(outs)
        except StopIteration as stop:
            pending[i], done[i] = None, stop.value

    while any(p is not None for p in pending):
        live = [i for i, p in enumerate(pending) if p is not None]
        group = live[:1]
        for i in live[1:]:
            if pending[i].kind != pending[group[0]].kind:
                group.append(i)
                break
        outs = _run([pending[i] for i in group], steps)
        for i, o in zip(group, outs):
            advance(i, o)
    return done


def _sigmoid_of_half(u):
    return 0.5 * jnp.tanh(u) + 0.5


def _silu_of_half(u):
    return u * jnp.tanh(u) + u


def _softplus(x):
    return jnp.maximum(x, 0.0) + jnp.log(1.0 + jnp.exp(-jnp.abs(x)))


def _dot(a, b):
    return jnp.dot(a, b, preferred_element_type=F32)


def _dot_nt(a, b):
    return lax.dot_general(a, b, (((1,), (1,)), ((), ())), preferred_element_type=F32)


def _dot_tn(a, b):
    return lax.dot_general(a, b, (((0,), (0,)), ((), ())), preferred_element_type=F32)


def _shift_rows(x, j, halo):
    n, c = x.shape
    x3 = x.reshape(n // SUBLANES, SUBLANES, c)
    prev = jnp.concatenate([halo[None], x3[:-1]], axis=0)
    sub = lax.broadcasted_iota(jnp.int32, x3.shape, 1)
    return pltpu.roll(jnp.where(sub >= SUBLANES - j, prev, x3), j, 1).reshape(n, c)


def _cumsum_rows(x):
    n = x.shape[0]
    row = lax.broadcasted_iota(jnp.int32, x.shape, 0)
    k = 1
    while k < n:
        x = x + jnp.where(row >= k, pltpu.roll(x, k, 0), 0.0)
        k *= 2
    return x


def _rmsnorm_rows(x, w):
    return x * lax.rsqrt(jnp.mean(x * x, axis=-1, keepdims=True) + RMS_EPS) * w


def _rope_table_kernel(inv_ref, cos_ref, sin_ref):
    rows = cos_ref.shape[0]
    base = pl.program_id(0) * rows
    pos = (base + lax.broadcasted_iota(jnp.int32, (rows, ATT_DH), 0)).astype(F32)
    lane = lax.broadcasted_iota(jnp.int32, (rows, ATT_DH), 1)
    ang = pos * inv_ref[...]
    cos_ref[...] = jnp.cos(ang)
    s = jnp.sin(ang)
    sin_ref[...] = jnp.where(lane < ATT_DH // 2, -s, s)


def _rope_table(seq):
    half = ATT_DH // 2
    inv = ROPE_THETA ** (-jnp.arange(half, dtype=F32) / half)
    inv = jnp.concatenate([inv, inv]).reshape(1, ATT_DH)
    rows = min(seq, 1024)
    return pl.pallas_call(
        _rope_table_kernel,
        grid=(seq // rows,),
        in_specs=[pl.BlockSpec((1, ATT_DH), lambda i: (0, 0))],
        out_specs=[pl.BlockSpec((rows, ATT_DH), lambda i: (i, 0))] * 2,
        out_shape=[jax.ShapeDtypeStruct((seq, ATT_DH), F32)] * 2,
        compiler_params=_params(("parallel",)),
        name="rope_table",
    )(inv)


def _norm_kernel(x_ref, w_ref, h_ref):
    h_ref[...] = _rmsnorm_rows(x_ref[...], w_ref[...]).astype(h_ref.dtype)


def _first_norm(x2d, w, tile0, seq):
    d = x2d.shape[1]
    return _Call("mxu", "first_norm", _norm_kernel, (x2d, w.astype(F32).reshape(1, d)),
                 (pl.BlockSpec((ROW_TILE, d), lambda i: (i + tile0, 0)), _const_spec((1, d))),
                 (pl.BlockSpec((ROW_TILE, d), lambda i: (i, 0)),), (jax.ShapeDtypeStruct((seq, d), BF16),))


def _ssd_proj_kernel(h_ref, w_ref, cw_ref, cb_ref, xc_ref, zs_ref, sm_ref, halo_ref):
    rows = h_ref.shape[0]

    h = h_ref[...]
    for j in range(SSD_CONV_DIM // PROJ_COLS):
        cols = slice(j * PROJ_COLS, (j + 1) * PROJ_COLS)
        acc = _dot(h, w_ref[:, cols])
        halo = halo_ref[:, cols]
        halo_ref[:, cols] = acc[rows - SUBLANES:, :]
        cw = cw_ref[:, cols]
        u = acc * cw[SSD_CONV - 1:SSD_CONV, :] + cb_ref[:, cols]
        for s in range(1, SSD_CONV):
            u = u + _shift_rows(acc, s, halo) * cw[SSD_CONV - 1 - s:SSD_CONV - s, :]
        xc_ref[:, cols] = _silu_of_half(u).astype(xc_ref.dtype)
    for j in range(SSD_INNER // PROJ_COLS):
        cols = slice(j * PROJ_COLS, (j + 1) * PROJ_COLS)
        wcols = slice(SSD_CONV_DIM + j * PROJ_COLS, SSD_CONV_DIM + (j + 1) * PROJ_COLS)
        zs_ref[:, cols] = _silu_of_half(_dot(h, w_ref[:, wcols])).astype(zs_ref.dtype)
    sm_ref[...] = _dot(h, w_ref[:, SSD_CONV_DIM + SSD_INNER:SSD_CONV_DIM + SSD_INNER + SMALL_COLS])


def _ssd_project(h, w, conv_w, conv_b):
    t, d = h.shape
    row = lambda i: (i, 0)
    return _Call(
        "mxu", "proj_ssd", _ssd_proj_kernel,
        (h, w, 0.5 * conv_w.astype(F32), 0.5 * conv_b.astype(F32).reshape(1, -1)),
        (pl.BlockSpec((ROW_TILE, d), row), _const_spec((d, W_BLOCK), W_SSD_BLOCK),
         _const_spec((SSD_CONV, SSD_CONV_DIM)), _const_spec((1, SSD_CONV_DIM))),
        (pl.BlockSpec((ROW_TILE, SSD_CONV_DIM), row), pl.BlockSpec((ROW_TILE, SSD_INNER), row),
         pl.BlockSpec((ROW_TILE, SMALL_COLS), row)),
        (jax.ShapeDtypeStruct((t, SSD_CONV_DIM), BF16), jax.ShapeDtypeStruct((t, SSD_INNER), BF16),
         jax.ShapeDtypeStruct((t, SMALL_COLS), F32)),
        (pltpu.VMEM((SUBLANES, SSD_CONV_DIM), F32),), zero_scratch=True)


def _gla_gates_proj_kernel(h_ref, w_ref, gla_ref, gt_ref):
    h = h_ref[...]
    n_gla = gla_ref.shape[1]
    for j in range(w_ref.shape[1] // PROJ_COLS):
        c0 = j * PROJ_COLS
        acc = _dot(h, w_ref[:, c0:c0 + PROJ_COLS])
        if c0 >= n_gla:
            gt_ref[:, c0 - n_gla:c0 - n_gla + PROJ_COLS] = _sigmoid_of_half(acc).astype(gt_ref.dtype)
        else:
            if c0 >= 2 * GLA_KEY + GLA_VAL:
                acc = _silu_of_half(acc)
            gla_ref[:, c0:c0 + PROJ_COLS] = acc.astype(gla_ref.dtype)


def _gla_gates_project(h, w):
    t, d = h.shape
    n_gla = 2 * GLA_KEY + 2 * GLA_VAL
    n_gates = W_BLOCK - n_gla
    row = lambda i: (i, 0)
    return _Call(
        "mxu", "proj_gla_gates", _gla_gates_proj_kernel, (h, w),
        (pl.BlockSpec((ROW_TILE, d), row), _const_spec((d, W_BLOCK), W_GLA_GATES_BLOCK)),
        (pl.BlockSpec((ROW_TILE, n_gla), row), pl.BlockSpec((ROW_TILE, n_gates), row)),
        (jax.ShapeDtypeStruct((t, n_gla), BF16), jax.ShapeDtypeStruct((t, n_gates), BF16)))


def _att_proj_kernel(h_ref, w_ref, cos_ref, sin_ref, *rest):
    out_refs, scr_ref = rest[:-1], rest[-1]
    h = h_ref[...]
    rows = h.shape[0]
    cos = cos_ref[...]
    sin = sin_ref[...]
    slot = 0
    for kind in range(3):
        for g, d in enumerate(ATT_DILATIONS):
            c0 = (kind * ATT_GROUPS + g) * ATT_OUT
            acc = _dot(h, w_ref[:, c0:c0 + ATT_OUT])
            o_ref = out_refs[kind * ATT_GROUPS + g]
            for a in range(ATT_HPG):
                xh = acc[:, a * ATT_DH:(a + 1) * ATT_DH]
                if kind < 2:
                    xh = xh * cos + pltpu.roll(xh, ATT_DH // 2, 1) * sin
                if kind == 0:
                    xh = xh * (ATT_DH ** -0.5)
                if d == 1:
                    o_ref[:, a * ATT_DH:(a + 1) * ATT_DH] = xh.astype(o_ref.dtype)
                    continue
                scr_ref[slot * rows:(slot + 1) * rows, :] = xh
                for r in range(d):
                    c = r * ATT_OUT + a * ATT_DH
                    o_ref[:, c:c + ATT_DH] = (
                        scr_ref[pl.ds(slot * rows + r, rows // d, stride=d), :].astype(o_ref.dtype))
                slot += 1


def _att_project(h, w_qkv, rope):
    t, dm = h.shape
    row = lambda i: (i, 0)
    rope_spec = pl.BlockSpec((ROW_TILE, ATT_DH), row)
    out_specs, out_shape = [], []
    for _ in range(3):
        for d in ATT_DILATIONS:
            out_specs.append(pl.BlockSpec((ROW_TILE // d, d * ATT_OUT), row))
            out_shape.append(jax.ShapeDtypeStruct((t // d, d * ATT_OUT), BF16))
    n_slots = 3 * sum(d > 1 for d in ATT_DILATIONS) * ATT_HPG
    return _Call(
        "mxu", "proj_att", _att_proj_kernel, (h, w_qkv, *rope),
        (pl.BlockSpec((ROW_TILE, dm), row), _const_spec((dm, W_BLOCK), W_ATT_BLOCK), rope_spec, rope_spec),
        tuple(out_specs), tuple(out_shape),
        (pltpu.VMEM((n_slots * ROW_TILE, ATT_DH), F32),))


def _ssd_kernel(xc_ref, zs_ref, sm_ref, dtb_ref, aneg_ref, dsk_ref, nw_ref, y_ref, st_ref):
    L = SSD_CHUNK
    pair = 2 * SSD_HEAD_DIM

    ti = lax.broadcasted_iota(jnp.int32, (L, L), 0)
    si = lax.broadcasted_iota(jnp.int32, (L, L), 1)
    causal = ti >= si
    low = lax.broadcasted_iota(jnp.int32, (L, pair), 1) < SSD_HEAD_DIM
    low_row = lax.broadcasted_iota(jnp.int32, (1, pair), 1) < SSD_HEAD_DIM
    for sub in range(xc_ref.shape[0] // L):
        rows = slice(sub * L, (sub + 1) * L)
        _ssd_chunk(xc_ref, zs_ref, sm_ref, dtb_ref, aneg_ref, dsk_ref, nw_ref, y_ref, st_ref, rows,
                   causal, low, low_row)


def _ssd_chunk(xc_ref, zs_ref, sm_ref, dtb_ref, aneg_ref, dsk_ref, nw_ref, y_ref, st_ref, rows,
               causal, low, low_row):
    L = SSD_CHUNK
    n_state = SSD_STATE
    pair = 2 * SSD_HEAD_DIM
    hpg = SSD_HEADS // SSD_GROUPS
    dt = _softplus(sm_ref[rows, :] + dtb_ref[...])
    a_cum = _cumsum_rows(dt * aneg_ref[...])
    a_last = a_cum[L - 1:L, :]
    w_end = jnp.exp(a_last - a_cum) * dt
    e_cum = jnp.exp(a_cum)
    chunk_decay = jnp.exp(a_last)
    a_cum_t = a_cum.T
    dt_t = dt.T

    b_off = SSD_INNER
    c_off = SSD_INNER + SSD_GROUPS * n_state
    y_blocks = []
    for p in range(SSD_HEADS // 2):
        g = (2 * p) // hpg
        if (2 * p) % hpg == 0:
            b_g = xc_ref[rows, b_off + g * n_state:b_off + (g + 1) * n_state]
            c_bf = xc_ref[rows, c_off + g * n_state:c_off + (g + 1) * n_state]
            cb = _dot_nt(c_bf, b_g)
            c_g = c_bf.astype(F32)
        x_pair = xc_ref[rows, p * pair:(p + 1) * pair]
        st = st_ref[p]
        lhs = []
        for h in (2 * p, 2 * p + 1):
            seg = a_cum[:, h:h + 1] - a_cum_t[h:h + 1, :]
            dec = jnp.exp(jnp.where(causal, seg, NEG_INF))
            m = (cb * dec * dt_t[h:h + 1, :]).astype(BF16)
            c_s = (c_g * e_cum[:, h:h + 1]).astype(BF16)
            lhs.append(jnp.concatenate([m, c_s], axis=1))
        yy = _dot(jnp.concatenate(lhs, axis=0), jnp.concatenate([x_pair, st.astype(BF16)], axis=0))
        x_f = x_pair.astype(F32)
        y_blocks.append(jnp.where(low, yy[:L], yy[L:]) + dsk_ref[:, p * pair:(p + 1) * pair] * x_f)

        w_pair = jnp.where(low, w_end[:, 2 * p:2 * p + 1], w_end[:, 2 * p + 1:2 * p + 2])
        cd = jnp.where(low_row, chunk_decay[:, 2 * p:2 * p + 1], chunk_decay[:, 2 * p + 1:2 * p + 2])
        st_ref[p] = st * cd + _dot_tn(b_g, (x_f * w_pair).astype(BF16))

    y = jnp.concatenate(y_blocks, axis=1)
    y = y * zs_ref[rows, :].astype(F32)
    gw = SSD_INNER // SSD_GROUPS
    nw = nw_ref[...]
    for g in range(SSD_GROUPS):
        cols = slice(g * gw, (g + 1) * gw)
        y_ref[rows, cols] = _rmsnorm_rows(y[:, cols], nw[:, cols]).astype(y_ref.dtype)


def _ssd_mixer(xc, zs, small, dt_bias, a_log, d_skip, norm_w):
    t = xc.shape[0]
    pad = SMALL_COLS - SSD_HEADS
    dtb = jnp.pad(dt_bias.astype(F32), (0, pad)).reshape(1, SMALL_COLS)
    aneg = jnp.pad(-jnp.exp(a_log.astype(F32)), (0, pad)).reshape(1, SMALL_COLS)
    dsk = jnp.repeat(d_skip.astype(F32), SSD_HEAD_DIM).reshape(1, SSD_INNER)
    row = lambda i: (i, 0)
    return _Call(
        "vec", "ssd_mixer", _ssd_kernel,
        (xc, zs, small, dtb, aneg, dsk, norm_w.astype(F32).reshape(1, -1)),
        (pl.BlockSpec((ROW_TILE, SSD_CONV_DIM), row), pl.BlockSpec((ROW_TILE, SSD_INNER), row),
         pl.BlockSpec((ROW_TILE, SMALL_COLS), row), _const_spec((1, SMALL_COLS)),
         _const_spec((1, SMALL_COLS)), _const_spec((1, SSD_INNER)), _const_spec((1, SSD_INNER))),
        (pl.BlockSpec((ROW_TILE, SSD_INNER), row),),
        (jax.ShapeDtypeStruct((t, SSD_INNER), BF16),),
        (pltpu.VMEM((SSD_HEADS // 2, SSD_STATE, 2 * SSD_HEAD_DIM), F32),), zero_scratch=True)


def _gla_level_ref_rows(b_ref, m, C):
    blk = 2 * m
    width = b_ref.shape[1]
    if blk >= SUBLANES:
        parts = []
        for s in range(0, C, blk):
            row = b_ref[s + m - 1:s + m, :]
            parts.append(jnp.broadcast_to(row, (blk, width)))
        return parts[0] if len(parts) == 1 else jnp.concatenate(parts, axis=0)
    b = b_ref[...]
    off = lax.broadcasted_iota(jnp.int32, (C, width), 0) % blk
    out = b
    for o in range(blk):
        delta = o - (m - 1)
        if delta == 0:
            continue
        out = jnp.where(off == o, pltpu.roll(b, delta % C, 0), out)
    return out


def _gla_kernel(qk_ref, v_ref, rs_ref, sm_ref, gw_ref, gb_ref, nw_ref, y_ref, st_ref, b_ref):
    C = GLA_CHUNK

    for sub in range(qk_ref.shape[0] // C):
        _gla_chunk(qk_ref, v_ref, rs_ref, sm_ref, gw_ref, gb_ref, nw_ref, y_ref, st_ref, b_ref.at[sub],
                   slice(sub * C, (sub + 1) * C))


def _gla_chunk(qk_ref, v_ref, rs_ref, sm_ref, gw_ref, gb_ref, nw_ref, y_ref, st_ref, b_ref, rows):
    C = GLA_CHUNK
    pair = 2 * GLA_DK
    qk = qk_ref[rows, :].astype(F32)
    q = qk[:, :GLA_KEY]
    k = qk[:, GLA_KEY:]

    s_f = sm_ref[rows, :]
    s_hi = s_f.astype(BF16)
    s_lo = (s_f - s_hi.astype(F32)).astype(BF16)
    g_f = gw_ref[...]
    g_hi = g_f.astype(BF16)
    g_lo = (g_f - g_hi.astype(F32)).astype(BF16)
    pre = _dot(s_hi, g_hi) + _dot(s_hi, g_lo) + _dot(s_lo, g_hi) + gb_ref[...]
    log_a = -_softplus(-pre) * (1.0 / GLA_TAU)
    b = _cumsum_rows(log_a)
    b_ref[...] = b
    b_last = b[C - 1:C, :]

    rowi = lax.broadcasted_iota(jnp.int32, (C, GLA_KEY), 0)
    lane = lax.broadcasted_iota(jnp.int32, (C, pair), 1)
    head_lanes = (lane < GLA_DK, lane >= GLA_DK)
    ti = lax.broadcasted_iota(jnp.int32, (C, C), 0)
    si = lax.broadcasted_iota(jnp.int32, (C, C), 1)
    ts_xor = ti ^ si

    head_mask = [jnp.where(sel, 1.0, 0.0).astype(BF16) for sel in head_lanes]

    def head_ops(qa, ka, h, mask_k=False):
        cols = slice((h // 2) * pair, (h // 2 + 1) * pair)
        qh = qa[:, cols] * head_mask[h % 2]
        kh = ka[:, cols] * head_mask[h % 2] if mask_k else ka[:, cols]
        return qh, kh

    def pair_scores(qa, ka, j):
        cols = slice(j * pair, (j + 1) * pair)
        qs = jnp.concatenate([qa[:, cols] * head_mask[0], qa[:, cols] * head_mask[1]], axis=0)
        return _dot_nt(qs, ka[:, cols])

    scores = []
    q_bf, k_bf = q.astype(BF16), k.astype(BF16)
    for j in range(GLA_HEADS // 2):
        p = pair_scores(q_bf, k_bf, j)
        scores += [jnp.where(ti == si, p[:C], 0.0), jnp.where(ti == si, p[C:], 0.0)]
    m = C // 2
    while m >= 1:
        blk = 2 * m
        upper = (rowi & m) != 0
        beta = _gla_level_ref_rows(b_ref, m, C)
        e = jnp.exp(jnp.where(upper, b - beta, beta - b))
        qt = jnp.where(upper, q * e, 0.0).astype(BF16)
        kt = jnp.where(upper, 0.0, k * e).astype(BF16)
        for j in range(GLA_HEADS // 2):
            p = pair_scores(qt, kt, j)
            for i, ph in enumerate((p[:C], p[C:])):
                if blk < C:
                    ph = jnp.where(ts_xor < blk, ph, 0.0)
                scores[2 * j + i] = scores[2 * j + i] + ph
        m //= 2

    qe = (q * jnp.exp(b)).astype(BF16)
    k_end = (k * jnp.exp(b_last - b)).astype(BF16)
    st_decay = jnp.exp(b_last)
    nw = nw_ref[...]
    for h in range(GLA_HEADS):
        j = h // 2
        vh = v_ref[rows, h * GLA_DV:(h + 1) * GLA_DV]
        st = st_ref[h]
        qh, kh = head_ops(qe, k_end, h, mask_k=True)
        o = _dot(scores[h].astype(BF16), vh) + _dot_nt(qh, st.astype(BF16))
        st_ref[h] = st * st_decay[:, j * pair:(j + 1) * pair] + _dot_tn(vh, kh)
        o = _rmsnorm_rows(o, nw) * rs_ref[rows, h * GLA_DV:(h + 1) * GLA_DV].astype(F32)
        y_ref[rows, h * GLA_DV:(h + 1) * GLA_DV] = o.astype(y_ref.dtype)


def _gla_mixer(gla_in, small, gate_w, gate_b, norm_w):
    t = gla_in.shape[0]
    gw = jnp.zeros((SMALL_COLS, GLA_KEY), F32).at[SSD_HEADS:SSD_HEADS + GLA_RANK].set(gate_w.astype(F32))
    row = lambda col: (lambda i: (i, col))
    return _Call(
        "vec", "gla_mixer", _gla_kernel,
        (gla_in, gla_in, gla_in, small, gw, gate_b.astype(F32).reshape(1, -1), norm_w.astype(F32).reshape(1, -1)),
        (pl.BlockSpec((ROW_TILE, 2 * GLA_KEY), row(0)), pl.BlockSpec((ROW_TILE, GLA_VAL), row(1)),
         pl.BlockSpec((ROW_TILE, GLA_VAL), row(2)), pl.BlockSpec((ROW_TILE, SMALL_COLS), row(0)),
         _const_spec((SMALL_COLS, GLA_KEY)), _const_spec((1, GLA_KEY)), _const_spec((1, GLA_DV))),
        (pl.BlockSpec((ROW_TILE, GLA_VAL), row(0)),),
        (jax.ShapeDtypeStruct((t, GLA_VAL), BF16),),
        (pltpu.VMEM((GLA_HEADS, GLA_DV, 2 * GLA_DK), F32),
         pltpu.VMEM((ROW_TILE // GLA_CHUNK, GLA_CHUNK, GLA_KEY), F32)),
        zero_scratch=True)


def _att_kernel(q_ref, kh_ref, k_ref, vh_ref, v_ref, o_ref, lse_ref, ktbuf_ref, vbuf_ref, *, nb):
    R = q_ref.shape[0]
    blk = ATT_BLOCK
    for h in range(ATT_HPG):
        cols = slice(h * ATT_DH, (h + 1) * ATT_DH)
        ktbuf_ref[cols, :blk] = kh_ref[:, cols].T
        ktbuf_ref[cols, blk:] = k_ref[:, cols].T
    vbuf_ref[:blk, :] = vh_ref[...]
    vbuf_ref[blk:, :] = v_ref[...]
    qi = lax.broadcasted_iota(jnp.int32, (blk, 2 * blk), 0)
    kj = lax.broadcasted_iota(jnp.int32, (blk, 2 * blk), 1)
    band = jnp.abs(2 * (kj - qi) - blk) <= blk
    first_key = jnp.where(pl.program_id(0) % nb > 0, 0, blk)
    lane = lax.broadcasted_iota(jnp.int32, (blk, LANES), 1)
    for sb in range(R // blk):
        rows = slice(sb * blk, (sb + 1) * blk)
        win = slice(sb * blk, (sb + 2) * blk)
        lse_tile = jnp.zeros((blk, LANES), F32)
        for h in range(ATT_HPG):
            cols = slice(h * ATT_DH, (h + 1) * ATT_DH)
            s = _dot(q_ref[rows, cols], ktbuf_ref[cols, win])
            s = jnp.where(band, s, NEG_INF)
            if sb == 0:
                s = jnp.where(kj >= first_key, s, NEG_INF)
            mx = jnp.max(s, axis=-1, keepdims=True)
            p = jnp.exp(s - mx)
            den = jnp.sum(p, axis=-1, keepdims=True)
            o = _dot(p.astype(BF16), vbuf_ref[win, cols])
            o_ref[rows, cols] = (o * (1.0 / den)).astype(o_ref.dtype)
            lse_tile = jnp.where(lane == h, mx + jnp.log(den), lse_tile)
        lse_ref[rows, :] = lse_tile


def _dilated_attention(q, k, v, g):
    d = ATT_DILATIONS[g]
    sub = q.shape[0]
    assert ATT_WINDOWS[g] // d == ATT_BLOCK and sub % ROW_TILE == 0
    nb = sub // ROW_TILE
    per = ROW_TILE // ATT_BLOCK
    main = pl.BlockSpec((ROW_TILE, ATT_OUT), lambda i: (i % nb, i // nb))
    halo = pl.BlockSpec((ATT_BLOCK, ATT_OUT), lambda i: (jnp.maximum((i % nb) * per - 1, 0), i // nb))
    return _Call(
        "vec", f"dilated_attention_d{d}", functools.partial(_att_kernel, nb=nb), (q, k, k, v, v),
        (main, halo, main, halo, main),
        (main, pl.BlockSpec((ROW_TILE, LANES), lambda i: (i % nb, i // nb))),
        (jax.ShapeDtypeStruct((sub, d * ATT_OUT), BF16), jax.ShapeDtypeStruct((sub, d * LANES), F32)),
        (pltpu.VMEM((ATT_OUT, ROW_TILE + ATT_BLOCK), BF16), pltpu.VMEM((ROW_TILE + ATT_BLOCK, ATT_OUT), BF16)))


def _interleave_rows(src_ref, scr_ref, d, rows, n_blk, base):
    if d == 1:
        return [src_ref[:, a * LANES:(a + 1) * LANES].astype(F32) for a in range(n_blk)]
    for r in range(d):
        for a in range(n_blk):
            c = (r * n_blk + a) * LANES
            scr_ref[pl.ds(base + a * rows + r, rows // d, stride=d), :] = src_ref[:, c:c + LANES].astype(F32)
    return [scr_ref[base + a * rows:base + (a + 1) * rows, :] for a in range(n_blk)]


def _merge_kernel(x_ref, ys_ref, yg_ref, o0_ref, o1_ref, o2_ref, l0_ref, l1_ref, l2_ref, gt_ref,
                  wb_ref, wo_ref, nw_ref, xo_ref, h_ref, oscr_ref, lscr_ref):
    rows = x_ref.shape[0]
    d = x_ref.shape[1]
    slots = [sum(dd > 1 for dd in ATT_DILATIONS[:g]) for g in range(ATT_GROUPS)]
    lses = [_interleave_rows(l_ref, lscr_ref, ATT_DILATIONS[g], rows, 1, slots[g] * rows)[0]
            for g, l_ref in enumerate((l0_ref, l1_ref, l2_ref))]
    mx = jnp.maximum(jnp.maximum(lses[0], lses[1]), lses[2])
    es = [jnp.exp(l - mx) for l in lses]
    inv = 1.0 / (es[0] + es[1] + es[2])
    acc = [jnp.zeros((rows, ATT_DH), F32) for _ in range(ATT_HPG)]
    for g, o_ref in enumerate((o0_ref, o1_ref, o2_ref)):
        o_g = _interleave_rows(o_ref, oscr_ref, ATT_DILATIONS[g], rows, ATT_HPG, slots[g] * ATT_HPG * rows)
        w_g = es[g] * inv
        for h in range(ATT_HPG):
            acc[h] = acc[h] + w_g[:, h:h + 1] * o_g[h]
    y_att = jnp.concatenate([a.astype(BF16) for a in acc], axis=1)

    merged = _sig_gate(gt_ref, 0, d) * _dot(ys_ref[...], wb_ref[:SSD_INNER, :])
    merged = merged + _sig_gate(gt_ref, 1, d) * _dot(yg_ref[...], wb_ref[SSD_INNER:SSD_INNER + GLA_VAL, :])
    merged = merged + _sig_gate(gt_ref, 2, d) * _dot(y_att, wb_ref[SSD_INNER + GLA_VAL:, :])
    x_new = x_ref[...] + _dot(merged.astype(BF16), wo_ref[...])
    xo_ref[...] = x_new
    h_ref[...] = _rmsnorm_rows(x_new, nw_ref[...]).astype(h_ref.dtype)


def _sig_gate(gt_ref, i, d):
    return gt_ref[:, i * d:(i + 1) * d].astype(F32)


def _merge(x2d, y_ssd, y_gla, att_o, att_lse, gates, w_branch, w_out, norm_w, tile0=0):
    t, d = y_ssd.shape[0], x2d.shape[1]
    row = lambda i: (i, 0)
    tile = lambda n: pl.BlockSpec((ROW_TILE, n), row)
    dil_tile = lambda n: [pl.BlockSpec((ROW_TILE // dil, dil * n), row) for dil in ATT_DILATIONS]
    n_dil = sum(dil > 1 for dil in ATT_DILATIONS)
    return _Call(
        "mxu", "merge_out", _merge_kernel,
        (x2d, y_ssd, y_gla, *att_o, *att_lse, gates, w_branch, w_out, norm_w.astype(F32).reshape(1, d)),
        (pl.BlockSpec((ROW_TILE, d), lambda i: (i + tile0, 0)), tile(SSD_INNER), tile(GLA_VAL),
         *dil_tile(ATT_OUT), *dil_tile(LANES), tile(3 * d),
         _const_spec(w_branch.shape), _const_spec(w_out.shape), _const_spec((1, d))),
        (tile(d), tile(d)),
        (jax.ShapeDtypeStruct((t, d), F32), jax.ShapeDtypeStruct((t, d), BF16)),
        (pltpu.VMEM((n_dil * ATT_HPG * ROW_TILE, ATT_DH), F32), pltpu.VMEM((n_dil * ROW_TILE, LANES), F32)))


def _ffn_kernel(x_ref, h_ref, wu_ref, cw_ref, cb_ref, wd_ref, nw_ref, xo_ref, ho_ref, halo_ref):
    rows = x_ref.shape[0]
    hid = wd_ref.shape[0]

    h = h_ref[...]
    gate = _dot(h, wu_ref[:, :hid])
    val = _dot(h, wu_ref[:, hid:])
    halo = halo_ref[...]
    halo_ref[...] = gate[rows - SUBLANES:, :]
    cw = cw_ref[...]
    u = gate * cw[FFN_CONV - 1:FFN_CONV, :] + cb_ref[...]
    for j in range(1, FFN_CONV):
        u = u + _shift_rows(gate, j, halo) * cw[FFN_CONV - 1 - j:FFN_CONV - j, :]
    act = (_silu_of_half(u) * val).astype(BF16)
    x_new = x_ref[...] + _dot(act, wd_ref[...])
    xo_ref[...] = x_new
    ho_ref[...] = _rmsnorm_rows(x_new, nw_ref[...]).astype(ho_ref.dtype)


def _conv_ffn(x2d, h, w_up, conv_w, conv_b, w_down, norm_w, out_dtype):
    t, d = x2d.shape
    hid = w_down.shape[0]
    tile = pl.BlockSpec((ROW_TILE, d), lambda i: (i, 0))
    return _Call(
        "mxu", "conv_ffn", _ffn_kernel,
        (x2d, h, w_up, 0.5 * conv_w.astype(F32), 0.5 * conv_b.astype(F32).reshape(1, hid), w_down,
         norm_w.astype(F32).reshape(1, d)),
        (tile, tile, _const_spec((d, 2 * hid)), _const_spec((FFN_CONV, hid)), _const_spec((1, hid)),
         _const_spec((hid, d)), _const_spec((1, d))),
        (tile, tile),
        (jax.ShapeDtypeStruct((t, d), F32), jax.ShapeDtypeStruct((t, d), out_dtype)),
        (pltpu.VMEM((SUBLANES, hid), F32),), zero_scratch=True)


def _in_proj_pieces(n_cols):
    widths = (SSD_INNER, SSD_CONV_DIM, SSD_HEADS, GLA_KEY, GLA_KEY, GLA_VAL, GLA_RANK, GLA_VAL,
              3 * ATT_HEADS * ATT_DH)
    starts = np.concatenate([[0], np.cumsum(widths)]).tolist()
    z, xbc, dt, gq, gk, gv, glr, gr, qkv, gates = [
        (starts[i], (widths + (n_cols - starts[-1],))[i]) for i in range(len(widths) + 1)]
    assert gates[1] + 2 * GLA_KEY + 2 * GLA_VAL == W_BLOCK and qkv[1] == W_BLOCK
    return [gq + (GLA_DK ** -0.5,), gk + (1.0,), gv + (1.0,), gr + (0.5,), gates + (0.5,), qkv + (1.0,),
            xbc + (1.0,), z + (0.5,), dt + (1.0,), glr + (1.0,)]


def _layout_kernel(w_ref, o_ref):
    rows = w_ref.shape[0]
    pieces = _in_proj_pieces(w_ref.shape[1])
    dst = 0
    for src, width, scale in pieces[:-2]:
        val = w_ref[:, src:src + width]
        o_ref[:, dst:dst + width] = (val if scale == 1.0 else val * scale).astype(o_ref.dtype)
        dst += width
    (dt_src, dt_w, _), (lr_src, lr_w, _) = pieces[-2:]
    small = jnp.concatenate([w_ref[:, dt_src:dt_src + dt_w], w_ref[:, lr_src:lr_src + lr_w],
                             jnp.zeros((rows, SMALL_COLS - dt_w - lr_w), F32)], axis=1)
    o_ref[:, dst:dst + SMALL_COLS] = small.astype(o_ref.dtype)
    dst += SMALL_COLS
    o_ref[:, dst:] = jnp.zeros((rows, o_ref.shape[1] - dst), o_ref.dtype)


def _layout_in_proj(w_in, layer):
    depth, d, n = w_in.shape
    rows = LANES
    tiles = d // rows
    return pl.pallas_call(
        _layout_kernel,
        grid=(tiles,),
        in_specs=[pl.BlockSpec((rows, n), lambda i: (layer * tiles + i, 0))],
        out_specs=pl.BlockSpec((rows, 3 * W_BLOCK), lambda i: (i, 0)),
        out_shape=jax.ShapeDtypeStruct((d, 3 * W_BLOCK), BF16),
        compiler_params=_params(("parallel",)),
        name="layout_in_proj",
    )(w_in.reshape(depth * d, n))


def _stream(x2d, tile0, seq, rope, layers, norm1_w, final_norm_w):
    (h,) = yield _first_norm(x2d, norm1_w[0], tile0, seq)
    x_s = x2d
    depth = len(layers)
    for l, p in enumerate(layers):
        xc, zs, small = yield _ssd_project(h, p["w_ssd"], p["ssd_conv_w"], p["ssd_conv_b"])
        gla_in, gates = yield _gla_gates_project(h, p["w_gla_gates"])
        att_in = yield _att_project(h, p["w_att"], rope)
        (y_ssd,) = yield _ssd_mixer(xc, zs, small, p["ssd_dt_bias"], p["ssd_a_log"], p["ssd_d"], p["ssd_norm_w"])
        (y_gla,) = yield _gla_mixer(gla_in, small, p["gla_gate_w"], p["gla_gate_b"], p["gla_norm_w"])
        att = []
        for g in range(ATT_GROUPS):
            att.append((yield _dilated_attention(att_in[g], att_in[ATT_GROUPS + g], att_in[2 * ATT_GROUPS + g], g)))
        x_s, h = yield _merge(x_s, y_ssd, y_gla, [a[0] for a in att], [a[1] for a in att], gates,
                              p["w_branch"], p["w_out"], p["norm2_w"], tile0 if l == 0 else 0)
        last = l == depth - 1
        x_s, h = yield _conv_ffn(x_s, h, p["ffn_up"], p["ffn_conv_w"], p["ffn_conv_b"], p["ffn_down"],
                                 final_norm_w if last else norm1_w[l + 1], F32 if last else BF16)
    return h


def kernel(x, norm1_w, w_in, ssd_conv_w, ssd_conv_b, ssd_dt_bias, ssd_a_log, ssd_d, ssd_norm_w,
           gla_gate_w, gla_gate_b, gla_norm_w, w_branch, w_out, norm2_w, ffn_up, ffn_conv_w,
           ffn_conv_b, ffn_down, final_norm_w):
    bsz, seq, d = x.shape
    depth = w_in.shape[0]
    assert seq % (ROW_TILE * max(ATT_DILATIONS)) == 0
    assert ROW_TILE % SSD_CHUNK == 0 and ROW_TILE % GLA_CHUNK == 0
    rope = _rope_table(seq)
    layers = []
    for l in range(depth):
        w = _layout_in_proj(w_in, l)
        layers.append({
            "w_ssd": w, "w_gla_gates": w, "w_att": w,
            "ssd_conv_w": ssd_conv_w[l], "ssd_conv_b": ssd_conv_b[l], "ssd_dt_bias": ssd_dt_bias[l],
            "ssd_a_log": ssd_a_log[l], "ssd_d": ssd_d[l], "ssd_norm_w": ssd_norm_w[l],
            "gla_gate_w": gla_gate_w[l], "gla_gate_b": gla_gate_b[l], "gla_norm_w": gla_norm_w[l],
            "w_branch": w_branch[l].astype(BF16), "w_out": w_out[l].astype(BF16), "norm2_w": norm2_w[l],
            "ffn_up": ffn_up[l].astype(BF16), "ffn_conv_w": ffn_conv_w[l], "ffn_conv_b": ffn_conv_b[l],
            "ffn_down": ffn_down[l].astype(BF16),
        })
    steps = seq // ROW_TILE
    x2d = x.reshape(bsz * seq, d)
    streams = [_stream(x2d, b * steps, seq, rope, layers, norm1_w, final_norm_w) for b in range(bsz)]
    return jnp.stack(_interleave(streams, steps))
```

```python
import functools
from typing import Callable, NamedTuple

import jax
import jax.numpy as jnp
import numpy as np
from jax import lax
from jax.experimental import pallas as pl
from jax.experimental.pallas import tpu as pltpu

F32 = jnp.float32
BF16 = jnp.bfloat16

LANES = 128
SUBLANES = 8
VMEM_LIMIT_BYTES = 56 * 1024 * 1024

RMS_EPS = 1e-6
SSD_HEADS = 16
SSD_HEAD_DIM = 64
SSD_INNER = SSD_HEADS * SSD_HEAD_DIM
SSD_STATE = 128
SSD_GROUPS = 4
SSD_CONV = 4
SSD_CONV_DIM = SSD_INNER + 2 * SSD_GROUPS * SSD_STATE
GLA_HEADS = 4
GLA_DK = 64
GLA_DV = 128
GLA_KEY = GLA_HEADS * GLA_DK
GLA_VAL = GLA_HEADS * GLA_DV
GLA_RANK = 16
GLA_TAU = 16.0
ATT_DILATIONS = (1, 4, 16)
ATT_WINDOWS = (128, 512, 2048)
ATT_GROUPS = 3
ATT_HPG = 4
ATT_HEADS = ATT_GROUPS * ATT_HPG
ATT_DH = 128
ATT_OUT = ATT_HPG * ATT_DH
ATT_BLOCK = 128
ROPE_THETA = 10000.0
FFN_CONV = 3

ROW_TILE = 512
PROJ_COLS = 512
SSD_CHUNK = 128
GLA_CHUNK = 128
SMALL_COLS = LANES
W_BLOCK = 3 * ATT_HEADS * ATT_DH
W_GLA_GATES_BLOCK, W_ATT_BLOCK, W_SSD_BLOCK = 0, 1, 2

NEG_INF = float("-inf")


def _params(sem):
    return pltpu.CompilerParams(dimension_semantics=sem, vmem_limit_bytes=VMEM_LIMIT_BYTES)


def _const_spec(shape, col_block=0):
    index = (0,) * (len(shape) - 1) + (col_block,)
    return pl.BlockSpec(shape, lambda *_: index, pipeline_mode=pl.Buffered(1))


class _Call(NamedTuple):
    kind: str
    name: str
    body: Callable
    args: tuple
    in_specs: tuple
    out_specs: tuple
    out_shape: tuple
    scratch_shapes: tuple = ()
    zero_scratch: bool = False


def _run(calls, steps):
    n_in = [len(c.args) for c in calls]
    n_out = [len(c.out_shape) for c in calls]
    n_scr = [len(c.scratch_shapes) for c in calls]

    def body(*refs):
        ins = refs[:sum(n_in)]
        outs = refs[sum(n_in):sum(n_in) + sum(n_out)]
        scrs = refs[sum(n_in) + sum(n_out):]

        @pl.when(pl.program_id(0) == 0)
        def _():
            s = 0
            for c, ns in zip(calls, n_scr):
                if c.zero_scratch:
                    for ref in scrs[s:s + ns]:
                        ref[...] = jnp.zeros_like(ref)
                s += ns

        i = o = s = 0
        for c, ni, no, ns in zip(calls, n_in, n_out, n_scr):
            c.body(*ins[i:i + ni], *outs[o:o + no], *scrs[s:s + ns])
            i, o, s = i + ni, o + no, s + ns

    flat = pl.pallas_call(
        body,
        grid=(steps,),
        in_specs=[sp for c in calls for sp in c.in_specs],
        out_specs=[sp for c in calls for sp in c.out_specs],
        out_shape=[sh for c in calls for sh in c.out_shape],
        scratch_shapes=[sc for c in calls for sc in c.scratch_shapes],
        compiler_params=_params(("arbitrary",)),
        name="__".join(c.name for c in calls),
    )(*[a for c in calls for a in c.args])
    outs, o = [], 0
    for no in n_out:
        outs.append(list(flat[o:o + no]))
        o += no
    return outs


def _interleave(streams, steps):
    pending = [next(st) for st in streams]
    done = [None] * len(streams)

    def advance(i, outs):
        try:
            pending[i] = streams[i].send(outs)
        except StopIteration as stop:
            pending[i], done[i] = None, stop.value

    while any(p is not None for p in pending):
        live = [i for i, p in enumerate(pending) if p is not None]
        group = live[:1]
        for i in live[1:]:
            if pending[i].kind != pending[group[0]].kind:
                group.append(i)
                break
        outs = _run([pending[i] for i in group], steps)
        for i, o in zip(group, outs):
            advance(i, o)
    return done


def _sigmoid_of_half(u):
    return 0.5 * jnp.tanh(u) + 0.5


def _silu_of_half(u):
    return u * jnp.tanh(u) + u


def _softplus(x):
    return jnp.maximum(x, 0.0) + jnp.log(1.0 + jnp.exp(-jnp.abs(x)))


def _dot(a, b):
    return jnp.dot(a, b, preferred_element_type=F32)


def _dot_nt(a, b):
    return lax.dot_general(a, b, (((1,), (1,)), ((), ())), preferred_element_type=F32)


def _dot_tn(a, b):
    return lax.dot_general(a, b, (((0,), (0,)), ((), ())), preferred_element_type=F32)


def _shift_rows(x, j, halo):
    n, c = x.shape
    x3 = x.reshape(n // SUBLANES, SUBLANES, c)
    prev = jnp.concatenate([halo[None], x3[:-1]], axis=0)
    sub = lax.broadcasted_iota(jnp.int32, x3.shape, 1)
    return pltpu.roll(jnp.where(sub >= SUBLANES - j, prev, x3), j, 1).reshape(n, c)


def _cumsum_rows(x):
    n = x.shape[0]
    row = lax.broadcasted_iota(jnp.int32, x.shape, 0)
    k = 1
    while k < n:
        x = x + jnp.where(row >= k, pltpu.roll(x, k, 0), 0.0)
        k *= 2
    return x


def _rmsnorm_rows(x, w):
    return x * lax.rsqrt(jnp.mean(x * x, axis=-1, keepdims=True) + RMS_EPS) * w


def _rope_table_kernel(inv_ref, cos_ref, sin_ref):
    rows = cos_ref.shape[0]
    base = pl.program_id(0) * rows
    pos = (base + lax.broadcasted_iota(jnp.int32, (rows, ATT_DH), 0)).astype(F32)
    lane = lax.broadcasted_iota(jnp.int32, (rows, ATT_DH), 1)
    ang = pos * inv_ref[...]
    cos_ref[...] = jnp.cos(ang)
    s = jnp.sin(ang)
    sin_ref[...] = jnp.where(lane < ATT_DH // 2, -s, s)


def _rope_table(seq):
    half = ATT_DH // 2
    inv = ROPE_THETA ** (-jnp.arange(half, dtype=F32) / half)
    inv = jnp.concatenate([inv, inv]).reshape(1, ATT_DH)
    rows = min(seq, 1024)
    return pl.pallas_call(
        _rope_table_kernel,
        grid=(seq // rows,),
        in_specs=[pl.BlockSpec((1, ATT_DH), lambda i: (0, 0))],
        out_specs=[pl.BlockSpec((rows, ATT_DH), lambda i: (i, 0))] * 2,
        out_shape=[jax.ShapeDtypeStruct((seq, ATT_DH), F32)] * 2,
        compiler_params=_params(("parallel",)),
        name="rope_table",
    )(inv)


def _norm_kernel(x_ref, w_ref, h_ref):
    h_ref[...] = _rmsnorm_rows(x_ref[...], w_ref[...]).astype(h_ref.dtype)


def _first_norm(x2d, w, tile0, seq):
    d = x2d.shape[1]
    return _Call("mxu", "first_norm", _norm_kernel, (x2d, w.astype(F32).reshape(1, d)),
                 (pl.BlockSpec((ROW_TILE, d), lambda i: (i + tile0, 0)), _const_spec((1, d))),
                 (pl.BlockSpec((ROW_TILE, d), lambda i: (i, 0)),), (jax.ShapeDtypeStruct((seq, d), BF16),))


def _ssd_proj_kernel(h_ref, w_ref, cw_ref, cb_ref, xc_ref, zs_ref, sm_ref, halo_ref):
    rows = h_ref.shape[0]

    h = h_ref[...]
    for j in range(SSD_CONV_DIM // PROJ_COLS):
        cols = slice(j * PROJ_COLS, (j + 1) * PROJ_COLS)
        acc = _dot(h, w_ref[:, cols])
        halo = halo_ref[:, cols]
        halo_ref[:, cols] = acc[rows - SUBLANES:, :]
        cw = cw_ref[:, cols]
        u = acc * cw[SSD_CONV - 1:SSD_CONV, :] + cb_ref[:, cols]
        for s in range(1, SSD_CONV):
            u = u + _shift_rows(acc, s, halo) * cw[SSD_CONV - 1 - s:SSD_CONV - s, :]
        xc_ref[:, cols] = _silu_of_half(u).astype(xc_ref.dtype)
    for j in range(SSD_INNER // PROJ_COLS):
        cols = slice(j * PROJ_COLS, (j + 1) * PROJ_COLS)
        wcols = slice(SSD_CONV_DIM + j * PROJ_COLS, SSD_CONV_DIM + (j + 1) * PROJ_COLS)
        zs_ref[:, cols] = _silu_of_half(_dot(h, w_ref[:, wcols])).astype(zs_ref.dtype)
    sm_ref[...] = _dot(h, w_ref[:, SSD_CONV_DIM + SSD_INNER:SSD_CONV_DIM + SSD_INNER + SMALL_COLS])


def _ssd_project(h, w, conv_w, conv_b):
    t, d = h.shape
    row = lambda i: (i, 0)
    return _Call(
        "mxu", "proj_ssd", _ssd_proj_kernel,
        (h, w, 0.5 * conv_w.astype(F32), 0.5 * conv_b.astype(F32).reshape(1, -1)),
        (pl.BlockSpec((ROW_TILE, d), row), _const_spec((d, W_BLOCK), W_SSD_BLOCK),
         _const_spec((SSD_CONV, SSD_CONV_DIM)), _const_spec((1, SSD_CONV_DIM))),
        (pl.BlockSpec((ROW_TILE, SSD_CONV_DIM), row), pl.BlockSpec((ROW_TILE, SSD_INNER), row),
         pl.BlockSpec((ROW_TILE, SMALL_COLS), row)),
        (jax.ShapeDtypeStruct((t, SSD_CONV_DIM), BF16), jax.ShapeDtypeStruct((t, SSD_INNER), BF16),
         jax.ShapeDtypeStruct((t, SMALL_COLS), F32)),
        (pltpu.VMEM((SUBLANES, SSD_CONV_DIM), F32),), zero_scratch=True)


def _gla_gates_proj_kernel(h_ref, w_ref, gla_ref, gt_ref):
    h = h_ref[...]
    n_gla = gla_ref.shape[1]
    for j in range(w_ref.shape[1] // PROJ_COLS):
        c0 = j * PROJ_COLS
        acc = _dot(h, w_ref[:, c0:c0 + PROJ_COLS])
        if c0 >= n_gla:
            gt_ref[:, c0 - n_gla:c0 - n_gla + PROJ_COLS] = _sigmoid_of_half(acc).astype(gt_ref.dtype)
        else:
            if c0 >= 2 * GLA_KEY + GLA_VAL:
                acc = _silu_of_half(acc)
            gla_ref[:, c0:c0 + PROJ_COLS] = acc.astype(gla_ref.dtype)


def _gla_gates_project(h, w):
    t, d = h.shape
    n_gla = 2 * GLA_KEY + 2 * GLA_VAL
    n_gates = W_BLOCK - n_gla
    row = lambda i: (i, 0)
    return _Call(
        "mxu", "proj_gla_gates", _gla_gates_proj_kernel, (h, w),
        (pl.BlockSpec((ROW_TILE, d), row), _const_spec((d, W_BLOCK), W_GLA_GATES_BLOCK)),
        (pl.BlockSpec((ROW_TILE, n_gla), row), pl.BlockSpec((ROW_TILE, n_gates), row)),
        (jax.ShapeDtypeStruct((t, n_gla), BF16), jax.ShapeDtypeStruct((t, n_gates), BF16)))


def _att_proj_kernel(h_ref, w_ref, cos_ref, sin_ref, *rest):
    out_refs, scr_ref = rest[:-1], rest[-1]
    h = h_ref[...]
    rows = h.shape[0]
    cos = cos_ref[...]
    sin = sin_ref[...]
    slot = 0
    for kind in range(3):
        for g, d in enumerate(ATT_DILATIONS):
            c0 = (kind * ATT_GROUPS + g) * ATT_OUT
            acc = _dot(h, w_ref[:, c0:c0 + ATT_OUT])
            o_ref = out_refs[kind * ATT_GROUPS + g]
            for a in range(ATT_HPG):
                xh = acc[:, a * ATT_DH:(a + 1) * ATT_DH]
                if kind < 2:
                    xh = xh * cos + pltpu.roll(xh, ATT_DH // 2, 1) * sin
                if kind == 0:
                    xh = xh * (ATT_DH ** -0.5)
                if d == 1:
                    o_ref[:, a * ATT_DH:(a + 1) * ATT_DH] = xh.astype(o_ref.dtype)
                    continue
                scr_ref[slot * rows:(slot + 1) * rows, :] = xh
                for r in range(d):
                    c = r * ATT_OUT + a * ATT_DH
                    o_ref[:, c:c + ATT_DH] = (
                        scr_ref[pl.ds(slot * rows + r, rows // d, stride=d), :].astype(o_ref.dtype))
                slot += 1


def _att_project(h, w_qkv, rope):
    t, dm = h.shape
    row = lambda i: (i, 0)
    rope_spec = pl.BlockSpec((ROW_TILE, ATT_DH), row)
    out_specs, out_shape = [], []
    for _ in range(3):
        for d in ATT_DILATIONS:
            out_specs.append(pl.BlockSpec((ROW_TILE // d, d * ATT_OUT), row))
            out_shape.append(jax.ShapeDtypeStruct((t // d, d * ATT_OUT), BF16))
    n_slots = 3 * sum(d > 1 for d in ATT_DILATIONS) * ATT_HPG
    return _Call(
        "mxu", "proj_att", _att_proj_kernel, (h, w_qkv, *rope),
        (pl.BlockSpec((ROW_TILE, dm), row), _const_spec((dm, W_BLOCK), W_ATT_BLOCK), rope_spec, rope_spec),
        tuple(out_specs), tuple(out_shape),
        (pltpu.VMEM((n_slots * ROW_TILE, ATT_DH), F32),))


def _ssd_kernel(xc_ref, zs_ref, sm_ref, dtb_ref, aneg_ref, dsk_ref, nw_ref, y_ref, st_ref):
    L = SSD_CHUNK
    pair = 2 * SSD_HEAD_DIM

    ti = lax.broadcasted_iota(jnp.int32, (L, L), 0)
    si = lax.broadcasted_iota(jnp.int32, (L, L), 1)
    causal = ti >= si
    low = lax.broadcasted_iota(jnp.int32, (L, pair), 1) < SSD_HEAD_DIM
    low_row = lax.broadcasted_iota(jnp.int32, (1, pair), 1) < SSD_HEAD_DIM
    for sub in range(xc_ref.shape[0] // L):
        rows = slice(sub * L, (sub + 1) * L)
        _ssd_chunk(xc_ref, zs_ref, sm_ref, dtb_ref, aneg_ref, dsk_ref, nw_ref, y_ref, st_ref, rows,
                   causal, low, low_row)


def _ssd_chunk(xc_ref, zs_ref, sm_ref, dtb_ref, aneg_ref, dsk_ref, nw_ref, y_ref, st_ref, rows,
               causal, low, low_row):
    L = SSD_CHUNK
    n_state = SSD_STATE
    pair = 2 * SSD_HEAD_DIM
    hpg = SSD_HEADS // SSD_GROUPS
    dt = _softplus(sm_ref[rows, :] + dtb_ref[...])
    a_cum = _cumsum_rows(dt * aneg_ref[...])
    a_last = a_cum[L - 1:L, :]
    w_end = jnp.exp(a_last - a_cum) * dt
    chunk_decay = jnp.exp(a_last)
    a_cum_t = a_cum.T
    dt_t = dt.T

    b_off = SSD_INNER
    c_off = SSD_INNER + SSD_GROUPS * n_state
    y_blocks = []
    for p in range(SSD_HEADS // 2):
        g = (2 * p) // hpg
        if (2 * p) % hpg == 0:
            b_g = xc_ref[rows, b_off + g * n_state:b_off + (g + 1) * n_state]
            c_bf = xc_ref[rows, c_off + g * n_state:c_off + (g + 1) * n_state]
            cb = _dot_nt(c_bf, b_g)
            c_g = c_bf.astype(F32)
        x_pair = xc_ref[rows, p * pair:(p + 1) * pair]
        st = st_ref[p]
        lhs = []
        for h in (2 * p, 2 * p + 1):
            a_col = jnp.broadcast_to(a_cum[:, h:h + 1], (L, L))
            seg = a_col - a_cum_t[h:h + 1, :]
            dec = jnp.exp(jnp.where(causal, seg, NEG_INF))
            m = (cb * dec * dt_t[h:h + 1, :]).astype(BF16)
            c_s = (c_g * jnp.exp(a_col)).astype(BF16)
            lhs.append(jnp.concatenate([m, c_s], axis=1))
        yy = _dot(jnp.concatenate(lhs, axis=0), jnp.concatenate([x_pair, st.astype(BF16)], axis=0))
        x_f = x_pair.astype(F32)
        y_blocks.append(jnp.where(low, yy[:L], yy[L:]) + dsk_ref[:, p * pair:(p + 1) * pair] * x_f)

        w_pair = jnp.where(low, w_end[:, 2 * p:2 * p + 1], w_end[:, 2 * p + 1:2 * p + 2])
        cd = jnp.where(low_row, chunk_decay[:, 2 * p:2 * p + 1], chunk_decay[:, 2 * p + 1:2 * p + 2])
        st_ref[p] = st * cd + _dot_tn(b_g, (x_f * w_pair).astype(BF16))

    y = jnp.concatenate(y_blocks, axis=1)
    y = y * zs_ref[rows, :].astype(F32)
    gw = SSD_INNER // SSD_GROUPS
    nw = nw_ref[...]
    for g in range(SSD_GROUPS):
        cols = slice(g * gw, (g + 1) * gw)
        y_ref[rows, cols] = _rmsnorm_rows(y[:, cols], nw[:, cols]).astype(y_ref.dtype)


def _ssd_mixer(xc, zs, small, dt_bias, a_log, d_skip, norm_w):
    t = xc.shape[0]
    pad = SMALL_COLS - SSD_HEADS
    dtb = jnp.pad(dt_bias.astype(F32), (0, pad)).reshape(1, SMALL_COLS)
    aneg = jnp.pad(-jnp.exp(a_log.astype(F32)), (0, pad)).reshape(1, SMALL_COLS)
    dsk = jnp.repeat(d_skip.astype(F32), SSD_HEAD_DIM).reshape(1, SSD_INNER)
    row = lambda i: (i, 0)
    return _Call(
        "vec", "ssd_mixer", _ssd_kernel,
        (xc, zs, small, dtb, aneg, dsk, norm_w.astype(F32).reshape(1, -1)),
        (pl.BlockSpec((ROW_TILE, SSD_CONV_DIM), row), pl.BlockSpec((ROW_TILE, SSD_INNER), row),
         pl.BlockSpec((ROW_TILE, SMALL_COLS), row), _const_spec((1, SMALL_COLS)),
         _const_spec((1, SMALL_COLS)), _const_spec((1, SSD_INNER)), _const_spec((1, SSD_INNER))),
        (pl.BlockSpec((ROW_TILE, SSD_INNER), row),),
        (jax.ShapeDtypeStruct((t, SSD_INNER), BF16),),
        (pltpu.VMEM((SSD_HEADS // 2, SSD_STATE, 2 * SSD_HEAD_DIM), F32),), zero_scratch=True)


def _gla_level_ref_rows(b_ref, m, C):
    blk = 2 * m
    width = b_ref.shape[1]
    if blk >= SUBLANES:
        parts = []
        for s in range(0, C, blk):
            row = b_ref[s + m - 1:s + m, :]
            parts.append(jnp.broadcast_to(row, (blk, width)))
        return parts[0] if len(parts) == 1 else jnp.concatenate(parts, axis=0)
    b = b_ref[...]
    off = lax.broadcasted_iota(jnp.int32, (C, width), 0) % blk
    out = b
    for o in range(blk):
        delta = o - (m - 1)
        if delta == 0:
            continue
        out = jnp.where(off == o, pltpu.roll(b, delta % C, 0), out)
    return out


def _gla_kernel(qk_ref, v_ref, rs_ref, sm_ref, gw_ref, gb_ref, nw_ref, y_ref, st_ref, b_ref):
    C = GLA_CHUNK

    for sub in range(qk_ref.shape[0] // C):
        _gla_chunk(qk_ref, v_ref, rs_ref, sm_ref, gw_ref, gb_ref, nw_ref, y_ref, st_ref, b_ref.at[sub],
                   slice(sub * C, (sub + 1) * C))


def _gla_chunk(qk_ref, v_ref, rs_ref, sm_ref, gw_ref, gb_ref, nw_ref, y_ref, st_ref, b_ref, rows):
    C = GLA_CHUNK
    pair = 2 * GLA_DK
    qk = qk_ref[rows, :].astype(F32)
    q = qk[:, :GLA_KEY]
    k = qk[:, GLA_KEY:]

    s_f = sm_ref[rows, :]
    s_hi = s_f.astype(BF16)
    s_lo = (s_f - s_hi.astype(F32)).astype(BF16)
    g_f = gw_ref[...]
    g_hi = g_f.astype(BF16)
    g_lo = (g_f - g_hi.astype(F32)).astype(BF16)
    pre = _dot(s_hi, g_hi) + _dot(s_hi, g_lo) + _dot(s_lo, g_hi) + gb_ref[...]
    log_a = -_softplus(-pre) * (1.0 / GLA_TAU)
    b = _cumsum_rows(log_a)
    b_ref[...] = b
    b_last = b[C - 1:C, :]

    rowi = lax.broadcasted_iota(jnp.int32, (C, GLA_KEY), 0)
    lane = lax.broadcasted_iota(jnp.int32, (C, pair), 1)
    head_lanes = (lane < GLA_DK, lane >= GLA_DK)
    ti = lax.broadcasted_iota(jnp.int32, (C, C), 0)
    si = lax.broadcasted_iota(jnp.int32, (C, C), 1)
    ts_xor = ti ^ si

    head_mask = [jnp.where(sel, 1.0, 0.0).astype(BF16) for sel in head_lanes]

    def head_ops(qa, ka, h, mask_k=False):
        cols = slice((h // 2) * pair, (h // 2 + 1) * pair)
        qh = qa[:, cols] * head_mask[h % 2]
        kh = ka[:, cols] * head_mask[h % 2] if mask_k else ka[:, cols]
        return qh, kh

    def pair_scores(qa, ka, j):
        cols = slice(j * pair, (j + 1) * pair)
        qs = jnp.concatenate([qa[:, cols] * head_mask[0], qa[:, cols] * head_mask[1]], axis=0)
        return _dot_nt(qs, ka[:, cols])

    scores = []
    q_bf, k_bf = q.astype(BF16), k.astype(BF16)
    for j in range(GLA_HEADS // 2):
        p = pair_scores(q_bf, k_bf, j)
        scores += [jnp.where(ti == si, p[:C], 0.0), jnp.where(ti == si, p[C:], 0.0)]
    m = C // 2
    while m >= 1:
        blk = 2 * m
        upper = (rowi & m) != 0
        beta = _gla_level_ref_rows(b_ref, m, C)
        e = jnp.exp(jnp.where(upper, b - beta, beta - b))
        qt = jnp.where(upper, q * e, 0.0).astype(BF16)
        kt = jnp.where(upper, 0.0, k * e).astype(BF16)
        for j in range(GLA_HEADS // 2):
            p = pair_scores(qt, kt, j)
            for i, ph in enumerate((p[:C], p[C:])):
                if blk < C:
                    ph = jnp.where(ts_xor < blk, ph, 0.0)
                scores[2 * j + i] = scores[2 * j + i] + ph
        m //= 2

    qe = (q * jnp.exp(b)).astype(BF16)
    k_end = (k * jnp.exp(b_last - b)).astype(BF16)
    st_decay = jnp.exp(b_last)
    nw = nw_ref[...]
    for h in range(GLA_HEADS):
        j = h // 2
        vh = v_ref[rows, h * GLA_DV:(h + 1) * GLA_DV]
        st = st_ref[h]
        qh, kh = head_ops(qe, k_end, h, mask_k=True)
        o = _dot(scores[h].astype(BF16), vh) + _dot_nt(qh, st.astype(BF16))
        st_ref[h] = st * st_decay[:, j * pair:(j + 1) * pair] + _dot_tn(vh, kh)
        o = _rmsnorm_rows(o, nw) * rs_ref[rows, h * GLA_DV:(h + 1) * GLA_DV].astype(F32)
        y_ref[rows, h * GLA_DV:(h + 1) * GLA_DV] = o.astype(y_ref.dtype)


def _gla_mixer(gla_in, small, gate_w, gate_b, norm_w):
    t = gla_in.shape[0]
    gw = jnp.zeros((SMALL_COLS, GLA_KEY), F32).at[SSD_HEADS:SSD_HEADS + GLA_RANK].set(gate_w.astype(F32))
    row = lambda col: (lambda i: (i, col))
    return _Call(
        "vec", "gla_mixer", _gla_kernel,
        (gla_in, gla_in, gla_in, small, gw, gate_b.astype(F32).reshape(1, -1), norm_w.astype(F32).reshape(1, -1)),
        (pl.BlockSpec((ROW_TILE, 2 * GLA_KEY), row(0)), pl.BlockSpec((ROW_TILE, GLA_VAL), row(1)),
         pl.BlockSpec((ROW_TILE, GLA_VAL), row(2)), pl.BlockSpec((ROW_TILE, SMALL_COLS), row(0)),
         _const_spec((SMALL_COLS, GLA_KEY)), _const_spec((1, GLA_KEY)), _const_spec((1, GLA_DV))),
        (pl.BlockSpec((ROW_TILE, GLA_VAL), row(0)),),
        (jax.ShapeDtypeStruct((t, GLA_VAL), BF16),),
        (pltpu.VMEM((GLA_HEADS, GLA_DV, 2 * GLA_DK), F32),
         pltpu.VMEM((ROW_TILE // GLA_CHUNK, GLA_CHUNK, GLA_KEY), F32)),
        zero_scratch=True)


def _att_kernel(q_ref, kh_ref, k_ref, vh_ref, v_ref, o_ref, lse_ref, ktbuf_ref, vbuf_ref, *, nb):
    R = q_ref.shape[0]
    blk = ATT_BLOCK
    for h in range(ATT_HPG):
        cols = slice(h * ATT_DH, (h + 1) * ATT_DH)
        ktbuf_ref[cols, :blk] = kh_ref[:, cols].T
        ktbuf_ref[cols, blk:] = k_ref[:, cols].T
    vbuf_ref[:blk, :] = vh_ref[...]
    vbuf_ref[blk:, :] = v_ref[...]
    qi = lax.broadcasted_iota(jnp.int32, (blk, 2 * blk), 0)
    kj = lax.broadcasted_iota(jnp.int32, (blk, 2 * blk), 1)
    band = jnp.abs(2 * (kj - qi) - blk) <= blk
    first_key = jnp.where(pl.program_id(0) % nb > 0, 0, blk)
    lane = lax.broadcasted_iota(jnp.int32, (blk, LANES), 1)
    for sb in range(R // blk):
        rows = slice(sb * blk, (sb + 1) * blk)
        win = slice(sb * blk, (sb + 2) * blk)
        lse_tile = jnp.zeros((blk, LANES), F32)
        for h in range(ATT_HPG):
            cols = slice(h * ATT_DH, (h + 1) * ATT_DH)
            s = _dot(q_ref[rows, cols], ktbuf_ref[cols, win])
            s = jnp.where(band, s, NEG_INF)
            if sb == 0:
                s = jnp.where(kj >= first_key, s, NEG_INF)
            mx = jnp.max(s, axis=-1, keepdims=True)
            p = jnp.exp(s - mx)
            den = jnp.sum(p, axis=-1, keepdims=True)
            o = _dot(p.astype(BF16), vbuf_ref[win, cols])
            o_ref[rows, cols] = (o * (1.0 / den)).astype(o_ref.dtype)
            lse_tile = jnp.where(lane == h, mx + jnp.log(den), lse_tile)
        lse_ref[rows, :] = lse_tile


def _dilated_attention(q, k, v, g):
    d = ATT_DILATIONS[g]
    sub = q.shape[0]
    assert ATT_WINDOWS[g] // d == ATT_BLOCK and sub % ROW_TILE == 0
    nb = sub // ROW_TILE
    per = ROW_TILE // ATT_BLOCK
    main = pl.BlockSpec((ROW_TILE, ATT_OUT), lambda i: (i % nb, i // nb))
    halo = pl.BlockSpec((ATT_BLOCK, ATT_OUT), lambda i: (jnp.maximum((i % nb) * per - 1, 0), i // nb))
    return _Call(
        "vec", f"dilated_attention_d{d}", functools.partial(_att_kernel, nb=nb), (q, k, k, v, v),
        (main, halo, main, halo, main),
        (main, pl.BlockSpec((ROW_TILE, LANES), lambda i: (i % nb, i // nb))),
        (jax.ShapeDtypeStruct((sub, d * ATT_OUT), BF16), jax.ShapeDtypeStruct((sub, d * LANES), F32)),
        (pltpu.VMEM((ATT_OUT, ROW_TILE + ATT_BLOCK), BF16), pltpu.VMEM((ROW_TILE + ATT_BLOCK, ATT_OUT), BF16)))


def _interleave_rows(src_ref, scr_ref, d, rows, n_blk, base):
    if d == 1:
        return [src_ref[:, a * LANES:(a + 1) * LANES].astype(F32) for a in range(n_blk)]
    for r in range(d):
        for a in range(n_blk):
            c = (r * n_blk + a) * LANES
            scr_ref[pl.ds(base + a * rows + r, rows // d, stride=d), :] = src_ref[:, c:c + LANES].astype(F32)
    return [scr_ref[base + a * rows:base + (a + 1) * rows, :] for a in range(n_blk)]


def _merge_kernel(x_ref, ys_ref, yg_ref, o0_ref, o1_ref, o2_ref, l0_ref, l1_ref, l2_ref, gt_ref,
                  wb_ref, wo_ref, nw_ref, xo_ref, h_ref, oscr_ref, lscr_ref):
    rows = x_ref.shape[0]
    d = x_ref.shape[1]
    slots = [sum(dd > 1 for dd in ATT_DILATIONS[:g]) for g in range(ATT_GROUPS)]
    lses = [_interleave_rows(l_ref, lscr_ref, ATT_DILATIONS[g], rows, 1, slots[g] * rows)[0]
            for g, l_ref in enumerate((l0_ref, l1_ref, l2_ref))]
    mx = jnp.maximum(jnp.maximum(lses[0], lses[1]), lses[2])
    es = [jnp.exp(l - mx) for l in lses]
    inv = 1.0 / (es[0] + es[1] + es[2])
    acc = [jnp.zeros((rows, ATT_DH), F32) for _ in range(ATT_HPG)]
    for g, o_ref in enumerate((o0_ref, o1_ref, o2_ref)):
        o_g = _interleave_rows(o_ref, oscr_ref, ATT_DILATIONS[g], rows, ATT_HPG, slots[g] * ATT_HPG * rows)
        w_g = es[g] * inv
        for h in range(ATT_HPG):
            acc[h] = acc[h] + w_g[:, h:h + 1] * o_g[h]
    y_att = jnp.concatenate([a.astype(BF16) for a in acc], axis=1)

    merged = _sig_gate(gt_ref, 0, d) * _dot(ys_ref[...], wb_ref[:SSD_INNER, :])
    merged = merged + _sig_gate(gt_ref, 1, d) * _dot(yg_ref[...], wb_ref[SSD_INNER:SSD_INNER + GLA_VAL, :])
    merged = merged + _sig_gate(gt_ref, 2, d) * _dot(y_att, wb_ref[SSD_INNER + GLA_VAL:, :])
    x_new = x_ref[...] + _dot(merged.astype(BF16), wo_ref[...])
    xo_ref[...] = x_new
    h_ref[...] = _rmsnorm_rows(x_new, nw_ref[...]).astype(h_ref.dtype)


def _sig_gate(gt_ref, i, d):
    return gt_ref[:, i * d:(i + 1) * d].astype(F32)


def _merge(x2d, y_ssd, y_gla, att_o, att_lse, gates, w_branch, w_out, norm_w, tile0=0):
    t, d = y_ssd.shape[0], x2d.shape[1]
    row = lambda i: (i, 0)
    tile = lambda n: pl.BlockSpec((ROW_TILE, n), row)
    dil_tile = lambda n: [pl.BlockSpec((ROW_TILE // dil, dil * n), row) for dil in ATT_DILATIONS]
    n_dil = sum(dil > 1 for dil in ATT_DILATIONS)
    return _Call(
        "mxu", "merge_out", _merge_kernel,
        (x2d, y_ssd, y_gla, *att_o, *att_lse, gates, w_branch, w_out, norm_w.astype(F32).reshape(1, d)),
        (pl.BlockSpec((ROW_TILE, d), lambda i: (i + tile0, 0)), tile(SSD_INNER), tile(GLA_VAL),
         *dil_tile(ATT_OUT), *dil_tile(LANES), tile(3 * d),
         _const_spec(w_branch.shape), _const_spec(w_out.shape), _const_spec((1, d))),
        (tile(d), tile(d)),
        (jax.ShapeDtypeStruct((t, d), F32), jax.ShapeDtypeStruct((t, d), BF16)),
        (pltpu.VMEM((n_dil * ATT_HPG * ROW_TILE, ATT_DH), F32), pltpu.VMEM((n_dil * ROW_TILE, LANES), F32)))


def _ffn_kernel(x_ref, h_ref, wu_ref, cw_ref, cb_ref, wd_ref, nw_ref, xo_ref, ho_ref, halo_ref):
    rows = x_ref.shape[0]
    hid = wd_ref.shape[0]

    h = h_ref[...]
    gate = _dot(h, wu_ref[:, :hid])
    val = _dot(h, wu_ref[:, hid:])
    halo = halo_ref[...]
    halo_ref[...] = gate[rows - SUBLANES:, :]
    cw = cw_ref[...]
    u = gate * cw[FFN_CONV - 1:FFN_CONV, :] + cb_ref[...]
    for j in range(1, FFN_CONV):
        u = u + _shift_rows(gate, j, halo) * cw[FFN_CONV - 1 - j:FFN_CONV - j, :]
    act = (_silu_of_half(u) * val).astype(BF16)
    x_new = x_ref[...] + _dot(act, wd_ref[...])
    xo_ref[...] = x_new
    ho_ref[...] = _rmsnorm_rows(x_new, nw_ref[...]).astype(ho_ref.dtype)


def _conv_ffn(x2d, h, w_up, conv_w, conv_b, w_down, norm_w, out_dtype):
    t, d = x2d.shape
    hid = w_down.shape[0]
    tile = pl.BlockSpec((ROW_TILE, d), lambda i: (i, 0))
    return _Call(
        "mxu", "conv_ffn", _ffn_kernel,
        (x2d, h, w_up, 0.5 * conv_w.astype(F32), 0.5 * conv_b.astype(F32).reshape(1, hid), w_down,
         norm_w.astype(F32).reshape(1, d)),
        (tile, tile, _const_spec((d, 2 * hid)), _const_spec((FFN_CONV, hid)), _const_spec((1, hid)),
         _const_spec((hid, d)), _const_spec((1, d))),
        (tile, tile),
        (jax.ShapeDtypeStruct((t, d), F32), jax.ShapeDtypeStruct((t, d), out_dtype)),
        (pltpu.VMEM((SUBLANES, hid), F32),), zero_scratch=True)


def _in_proj_pieces(n_cols):
    widths = (SSD_INNER, SSD_CONV_DIM, SSD_HEADS, GLA_KEY, GLA_KEY, GLA_VAL, GLA_RANK, GLA_VAL,
              3 * ATT_HEADS * ATT_DH)
    starts = np.concatenate([[0], np.cumsum(widths)]).tolist()
    z, xbc, dt, gq, gk, gv, glr, gr, qkv, gates = [
        (starts[i], (widths + (n_cols - starts[-1],))[i]) for i in range(len(widths) + 1)]
    assert gates[1] + 2 * GLA_KEY + 2 * GLA_VAL == W_BLOCK and qkv[1] == W_BLOCK
    return [gq + (GLA_DK ** -0.5,), gk + (1.0,), gv + (1.0,), gr + (0.5,), gates + (0.5,), qkv + (1.0,),
            xbc + (1.0,), z + (0.5,), dt + (1.0,), glr + (1.0,)]


def _layout_kernel(w_ref, o_ref):
    rows = w_ref.shape[0]
    pieces = _in_proj_pieces(w_ref.shape[1])
    dst = 0
    for src, width, scale in pieces[:-2]:
        val = w_ref[:, src:src + width]
        o_ref[:, dst:dst + width] = (val if scale == 1.0 else val * scale).astype(o_ref.dtype)
        dst += width
    (dt_src, dt_w, _), (lr_src, lr_w, _) = pieces[-2:]
    small = jnp.concatenate([w_ref[:, dt_src:dt_src + dt_w], w_ref[:, lr_src:lr_src + lr_w],
                             jnp.zeros((rows, SMALL_COLS - dt_w - lr_w), F32)], axis=1)
    o_ref[:, dst:dst + SMALL_COLS] = small.astype(o_ref.dtype)
    dst += SMALL_COLS
    o_ref[:, dst:] = jnp.zeros((rows, o_ref.shape[1] - dst), o_ref.dtype)


def _layout_in_proj(w_in, layer):
    _, d, n = w_in.shape
    rows = LANES
    return pl.pallas_call(
        _layout_kernel,
        grid=(d // rows,),
        in_specs=[pl.BlockSpec((None, rows, n), lambda i: (layer, i, 0))],
        out_specs=pl.BlockSpec((rows, 3 * W_BLOCK), lambda i: (i, 0)),
        out_shape=jax.ShapeDtypeStruct((d, 3 * W_BLOCK), BF16),
        compiler_params=_params(("parallel",)),
        name="layout_in_proj",
    )(w_in)


def _stream(x2d, tile0, seq, rope, layers, norm1_w, final_norm_w):
    (h,) = yield _first_norm(x2d, norm1_w[0], tile0, seq)
    x_s = x2d
    depth = len(layers)
    for l, p in enumerate(layers):
        xc, zs, small = yield _ssd_project(h, p["w_ssd"], p["ssd_conv_w"], p["ssd_conv_b"])
        gla_in, gates = yield _gla_gates_project(h, p["w_gla_gates"])
        att_in = yield _att_project(h, p["w_att"], rope)
        (y_ssd,) = yield _ssd_mixer(xc, zs, small, p["ssd_dt_bias"], p["ssd_a_log"], p["ssd_d"], p["ssd_norm_w"])
        (y_gla,) = yield _gla_mixer(gla_in, small, p["gla_gate_w"], p["gla_gate_b"], p["gla_norm_w"])
        att = []
        for g in range(ATT_GROUPS):
            att.append((yield _dilated_attention(att_in[g], att_in[ATT_GROUPS + g], att_in[2 * ATT_GROUPS + g], g)))
        x_s, h = yield _merge(x_s, y_ssd, y_gla, [a[0] for a in att], [a[1] for a in att], gates,
                              p["w_branch"], p["w_out"], p["norm2_w"], tile0 if l == 0 else 0)
        last = l == depth - 1
        x_s, h = yield _conv_ffn(x_s, h, p["ffn_up"], p["ffn_conv_w"], p["ffn_conv_b"], p["ffn_down"],
                                 final_norm_w if last else norm1_w[l + 1], F32 if last else BF16)
    return h


def kernel(x, norm1_w, w_in, ssd_conv_w, ssd_conv_b, ssd_dt_bias, ssd_a_log, ssd_d, ssd_norm_w,
           gla_gate_w, gla_gate_b, gla_norm_w, w_branch, w_out, norm2_w, ffn_up, ffn_conv_w,
           ffn_conv_b, ffn_down, final_norm_w):
    bsz, seq, d = x.shape
    depth = w_in.shape[0]
    assert seq % (ROW_TILE * max(ATT_DILATIONS)) == 0
    assert ROW_TILE % SSD_CHUNK == 0 and ROW_TILE % GLA_CHUNK == 0
    rope = _rope_table(seq)
    layers = []
    for l in range(depth):
        w = _layout_in_proj(w_in, l)
        layers.append({
            "w_ssd": w, "w_gla_gates": w, "w_att": w,
            "ssd_conv_w": ssd_conv_w[l], "ssd_conv_b": ssd_conv_b[l], "ssd_dt_bias": ssd_dt_bias[l],
            "ssd_a_log": ssd_a_log[l], "ssd_d": ssd_d[l], "ssd_norm_w": ssd_norm_w[l],
            "gla_gate_w": gla_gate_w[l], "gla_gate_b": gla_gate_b[l], "gla_norm_w": gla_norm_w[l],
            "w_branch": w_branch[l].astype(BF16), "w_out": w_out[l].astype(BF16), "norm2_w": norm2_w[l],
            "ffn_up": ffn_up[l].astype(BF16), "ffn_conv_w": ffn_conv_w[l], "ffn_conv_b": ffn_conv_b[l],
            "ffn_down": ffn_down[l].astype(BF16),
        })
    steps = seq // ROW_TILE
    x2d = x.reshape(bsz * seq, d)
    streams = [_stream(x2d, b * steps, seq, rope, layers, norm1_w, final_norm_w) for b in range(bsz)]
    return jnp.stack(_interleave(streams, steps))
```
